```python
import math
import jax
import jax.numpy as jnp
from jax import lax
import numpy as np

D_MODEL = 2048
BATCH = 4
SEQ = 4096
DEPTH = 4

GRID_W = 64
CTX_LEN = 256

N_HEADS = 8
N_KV_HEADS = 2
HEAD_DIM = 128
Q_GROUP = N_HEADS // N_KV_HEADS
ATTN_WIDTH = N_HEADS * HEAD_DIM
KV_WIDTH = N_KV_HEADS * HEAD_DIM
WINDOW = 128
ATTN_BLOCK = 128
ROPE_THETA = 10000.0
MASK_VALUE = -1e30

HYENA_WIDTH = 512
HYENA_SHORT = 3
HYENA_BANDS = 16
HYENA_EMB = 1 + 2 * HYENA_BANDS
HYENA_FILTER_HIDDEN = 64
HYENA_TARGET = 1e-2
HYENA_DECAY_PCT_HI = 0.3
HYENA_DECAY_PCT_LO = 1.5
HYENA_MAX_DECAY = math.log(HYENA_TARGET) / HYENA_DECAY_PCT_HI
HYENA_MIN_DECAY = math.log(HYENA_TARGET) / HYENA_DECAY_PCT_LO

FNET_WIDTH = 512
FNET_GROUPS = 4
FNET_GROUP_DIM = FNET_WIDTH // FNET_GROUPS

N_BRANCHES = 3

Q_OFF = 0
K_OFF = Q_OFF + ATTN_WIDTH
V_OFF = K_OFF + KV_WIDTH
HY_OFF = V_OFF + KV_WIDTH
FN_OFF = HY_OFF + 3 * HYENA_WIDTH
GATE_OFF = FN_OFF + FNET_WIDTH
IN_WIDTH = GATE_OFF + N_BRANCHES * D_MODEL

N_GROUPS = 4
EXPERTS_PER_GROUP = 8
N_EXPERTS = N_GROUPS * EXPERTS_PER_GROUP
TOP_K = 2
EXPERT_HIDDEN = 512
MOE_BLOCK = 128

LN_EPS = 1e-6
DEEPNORM_ALPHA = (2 * DEPTH) ** 0.25
DEEPNORM_BETA = (8 * DEPTH) ** -0.25

kernel_name = "hybrid_hyena_fnet_swa_hmoe_dit_trunk"


def layer_norm(x):
    xf = x.astype(jnp.float32)
    mu = jnp.mean(xf, -1, keepdims=True)
    var = jnp.mean(jnp.square(xf - mu), -1, keepdims=True)
    return ((xf - mu) * lax.rsqrt(var + LN_EPS)).astype(x.dtype)


def modulate(x, shift, scale):
    return layer_norm(x) * (1 + scale) + shift


def post_ln(x, g, b):
    return layer_norm(x) * g + b


def axial_rope_angles(seq_len):
    rows = seq_len // GRID_W
    row = jnp.broadcast_to(jnp.arange(rows)[:, None], (rows, GRID_W)).reshape(-1)
    col = jnp.broadcast_to(jnp.arange(GRID_W)[None, :], (rows, GRID_W)).reshape(-1)
    n_freq = HEAD_DIM // 4
    inv_freq = ROPE_THETA ** (-jnp.arange(n_freq, dtype=jnp.float32) / n_freq)
    ang_row = row.astype(jnp.float32)[:, None] * inv_freq[None, :]
    ang_col = col.astype(jnp.float32)[:, None] * inv_freq[None, :]
    return ang_row, ang_col


def _rotate(x, ang):
    n = ang.shape[-1]
    cos = jnp.cos(ang)[None, :, None, :]
    sin = jnp.sin(ang)[None, :, None, :]
    x1, x2 = x[..., :n], x[..., n:]
    return jnp.concatenate([x1 * cos - x2 * sin, x2 * cos + x1 * sin], -1)


def apply_axial_rope(x, ang_row, ang_col):
    half = HEAD_DIM // 2
    xf = x.astype(jnp.float32)
    out = jnp.concatenate([_rotate(xf[..., :half], ang_row), _rotate(xf[..., half:], ang_col)], -1)
    return out.astype(x.dtype)


def sink_probs(logits, sink):
    m = jnp.maximum(jnp.max(logits, -1, keepdims=True), sink)
    e = jnp.exp(logits - m)
    return e / (jnp.sum(e, -1, keepdims=True) + jnp.exp(sink - m))


def latent_window_attention(q, k, v, kc, vc, sink):
    b_, s_, _, _ = q.shape
    nb = s_ // ATTN_BLOCK
    scale = HEAD_DIM ** -0.5
    qb = q.reshape(b_, nb, ATTN_BLOCK, N_KV_HEADS, Q_GROUP, HEAD_DIM)

    def windows(t):
        tb = t.reshape(b_, nb, ATTN_BLOCK, N_KV_HEADS, HEAD_DIM)
        tp = jnp.pad(tb, ((0, 0), (1, 1), (0, 0), (0, 0), (0, 0)))
        return jnp.concatenate([tp[:, :-2], tp[:, 1:-1], tp[:, 2:]], axis=2)

    kw, vw = windows(k), windows(v)
    s_loc = jnp.einsum('bnqhgd,bnjhd->bnhgqj', qb, kw, preferred_element_type=jnp.float32) * scale
    s_ctx = jnp.einsum('bnqhgd,bchd->bnhgqc', qb, kc, preferred_element_type=jnp.float32) * scale
    blk = jnp.arange(nb)[:, None, None]
    qpos = blk * ATTN_BLOCK + jnp.arange(ATTN_BLOCK)[None, :, None]
    kpos = (blk - 1) * ATTN_BLOCK + jnp.arange(3 * ATTN_BLOCK)[None, None, :]
    valid = (jnp.abs(qpos - kpos) <= WINDOW) & (kpos >= 0) & (kpos < s_)
    s_loc = jnp.where(valid[None, :, None, None], s_loc, MASK_VALUE)
    sink_b = sink.astype(jnp.float32).reshape(1, 1, N_KV_HEADS, Q_GROUP, 1, 1)
    p = sink_probs(jnp.concatenate([s_loc, s_ctx], -1), sink_b)
    p_loc = p[..., :3 * ATTN_BLOCK].astype(v.dtype)
    p_ctx = p[..., 3 * ATTN_BLOCK:].astype(vc.dtype)
    o = (jnp.einsum('bnhgqj,bnjhd->bnqhgd', p_loc, vw)
         + jnp.einsum('bnhgqc,bchd->bnqhgd', p_ctx, vc))
    return o.reshape(b_, s_, ATTN_WIDTH)


def context_attention(qc, kc, vc, sink):
    b_, c_, _, _ = qc.shape
    scale = HEAD_DIM ** -0.5
    qg = qc.reshape(b_, c_, N_KV_HEADS, Q_GROUP, HEAD_DIM)
    s = jnp.einsum('bqhgd,bchd->bhgqc', qg, kc, preferred_element_type=jnp.float32) * scale
    p = sink_probs(s, sink.astype(jnp.float32).reshape(1, N_KV_HEADS, Q_GROUP, 1, 1))
    o = jnp.einsum('bhgqc,bchd->bqhgd', p.astype(vc.dtype), vc)
    return o.reshape(b_, c_, ATTN_WIDTH)


def short_conv(u, w, b):
    n = u.shape[1]
    pad = HYENA_SHORT // 2
    up = jnp.pad(u, ((0, 0), (pad, pad), (0, 0)))
    out = b
    for j in range(HYENA_SHORT):
        out = out + up[:, j:j + n] * w[j]
    return out


def hyena_filter(n, w1, b1, w2, b2, w3, b3, w4, b4, freq):
    f32 = jnp.float32
    t = jnp.linspace(0.0, 1.0, n, dtype=f32)[:, None]
    w = 2.0 * math.pi * jnp.arange(n, dtype=f32)[:, None] / n
    bands = jnp.linspace(1e-4, HYENA_BANDS - 1, HYENA_BANDS, dtype=f32)[None, :]
    z = jnp.concatenate([t, jnp.cos(bands * w), -jnp.sin(bands * w)], -1)
    fr = freq.astype(f32)
    hdn = jnp.sin(fr[0] * (z @ w1.astype(f32) + b1.astype(f32)))
    hdn = jnp.sin(fr[1] * (hdn @ w2.astype(f32) + b2.astype(f32)))
    hdn = jnp.sin(fr[2] * (hdn @ w3.astype(f32) + b3.astype(f32)))
    filt = (hdn @ w4.astype(f32) + b4.astype(f32)).reshape(n, 2, HYENA_WIDTH)
    deltas = jnp.abs(jnp.linspace(HYENA_MIN_DECAY, HYENA_MAX_DECAY, HYENA_WIDTH, dtype=f32))
    filt = filt * jnp.exp(-t * deltas[None, :])[:, None, :]
    circ = jnp.concatenate([filt[:, 0], jnp.zeros((1, HYENA_WIDTH), f32), filt[:0:-1, 1]], 0)
    return circ / (jnp.sum(jnp.abs(circ), 0, keepdims=True) + 1e-6)


def hyena_branch(u, conv_w, conv_b, filt_params, d_bias):
    n = u.shape[1]
    uc = short_conv(u, conv_w, conv_b)
    x0, x1, v = jnp.split(uc, 3, axis=-1)
    circ = hyena_filter(n, *filt_params)
    z = (x1 * v).astype(jnp.float32)
    zf = jnp.fft.rfft(z, n=2 * n, axis=1)
    hf = jnp.fft.rfft(circ, n=2 * n, axis=0)
    y = jnp.fft.irfft(zf * hf[None], n=2 * n, axis=1)[:, :n] + z * d_bias.astype(jnp.float32)
    return x0 * y.astype(u.dtype)


def fnet_branch(u):
    b_, n, _ = u.shape
    ug = u.astype(jnp.float32).reshape(b_, n, FNET_GROUPS, FNET_GROUP_DIM)
    y = jnp.fft.fftn(ug, axes=(1, 3), norm='ortho').real
    return y.reshape(b_, n, FNET_WIDTH).astype(u.dtype)


def mixer_output(p, y_attn, conv_w, conv_b, filt_params, d_bias, w_ba, w_bh, w_bf, w_o):
    y_hy = hyena_branch(p[..., HY_OFF:FN_OFF], conv_w, conv_b, filt_params, d_bias)
    y_fn = fnet_branch(p[..., FN_OFF:GATE_OFF])
    g = jax.nn.sigmoid(p[..., GATE_OFF:].astype(jnp.float32)).astype(p.dtype)
    g_a, g_h, g_f = jnp.split(g, N_BRANCHES, axis=-1)
    y = g_a * (y_attn @ w_ba) + g_h * (y_hy @ w_bh) + g_f * (y_fn @ w_bf)
    return y @ w_o


def hier_route(h, wg, bg, we, be):
    n_tok = h.shape[0]
    g_logits = (h @ wg + bg).astype(jnp.float32)
    g_prob = jax.nn.softmax(g_logits, -1)
    grp = jnp.argmax(g_logits, -1)
    rows = jnp.arange(n_tok)
    g_w = g_prob[rows, grp][:, None]
    e_logits = (h @ we + be).astype(jnp.float32).reshape(n_tok, N_GROUPS, EXPERTS_PER_GROUP)
    e_in = e_logits[rows, grp]
    top_v, top_i = lax.top_k(e_in, TOP_K)
    weights = jax.nn.softmax(top_v, -1) * g_w
    expert_id = grp[:, None] * EXPERTS_PER_GROUP + top_i
    return expert_id, weights


def moe_forward(h, expert_id, weights, w1, w3, w2):
    n_tok, d = h.shape
    n_asg = n_tok * TOP_K
    e_flat = expert_id.reshape(-1)
    tok = jnp.repeat(jnp.arange(n_tok, dtype=jnp.int32), TOP_K)
    w_flat = weights.reshape(-1)
    order = jnp.argsort(e_flat)
    e_s, tok_s, w_s = e_flat[order], tok[order], w_flat[order]
    counts = jnp.zeros((N_EXPERTS,), jnp.int32).at[e_flat].add(1)
    padded = ((counts + MOE_BLOCK - 1) // MOE_BLOCK) * MOE_BLOCK
    start = jnp.cumsum(counts) - counts
    pend = jnp.cumsum(padded)
    pstart = pend - padded
    dest = pstart[e_s] + (jnp.arange(n_asg, dtype=jnp.int32) - start[e_s])
    n_blocks = -(-(n_asg + N_EXPERTS * (MOE_BLOCK - 1)) // MOE_BLOCK)
    n_slots = n_blocks * MOE_BLOCK
    slot_tok = jnp.full((n_slots,), n_tok, jnp.int32).at[dest].set(tok_s)
    slot_w = jnp.zeros((n_slots,), h.dtype).at[dest].set(w_s.astype(h.dtype))
    block_e = jnp.minimum(jnp.searchsorted(pend, jnp.arange(n_blocks, dtype=jnp.int32) * MOE_BLOCK, side='right'), N_EXPERTS - 1)
    h_pad = jnp.concatenate([h, jnp.zeros((1, d), h.dtype)], 0)
    xb = h_pad[slot_tok].reshape(n_blocks, MOE_BLOCK, d)

    def expert_block(args):
        xblk, e = args
        return (jax.nn.silu(xblk @ w1[e]) * (xblk @ w3[e])) @ w2[e]

    yb = lax.map(expert_block, (xb, block_e))
    out = jnp.zeros((n_tok + 1, d), h.dtype).at[slot_tok].add(yb.reshape(n_slots, d) * slot_w[:, None])
    return out[:n_tok]


def setup_inputs(seed: int = 0) -> dict:
    key = jax.random.key(seed)
    ks = iter(jax.random.split(key, 48))

    def nrm(shape, scale):
        return jax.random.normal(next(ks), shape, jnp.float32) * scale

    d = D_MODEL
    return {
        'x': nrm((BATCH, SEQ, d), 1.0),
        'c': nrm((BATCH, d), 1.0),
        'ctx': nrm((BATCH, CTX_LEN, d), 1.0),
        'c_ctx': nrm((d,), 1.0),
        'w_ada': nrm((DEPTH, d, 6 * d), 0.5 * d ** -0.5),
        'b_ada': nrm((DEPTH, 6 * d), 0.01),
        'w_in': nrm((DEPTH, d, IN_WIDTH), d ** -0.5),
        'conv_w': nrm((DEPTH, HYENA_SHORT, 3 * HYENA_WIDTH), HYENA_SHORT ** -0.5),
        'conv_b': nrm((DEPTH, 3 * HYENA_WIDTH), 0.01),
        'hy_w1': nrm((DEPTH, HYENA_EMB, HYENA_FILTER_HIDDEN), HYENA_EMB ** -0.5),
        'hy_b1': nrm((DEPTH, HYENA_FILTER_HIDDEN), 0.01),
        'hy_w2': nrm((DEPTH, HYENA_FILTER_HIDDEN, HYENA_FILTER_HIDDEN), HYENA_FILTER_HIDDEN ** -0.5),
        'hy_b2': nrm((DEPTH, HYENA_FILTER_HIDDEN), 0.01),
        'hy_w3': nrm((DEPTH, HYENA_FILTER_HIDDEN, HYENA_FILTER_HIDDEN), HYENA_FILTER_HIDDEN ** -0.5),
        'hy_b3': nrm((DEPTH, HYENA_FILTER_HIDDEN), 0.01),
        'hy_w4': nrm((DEPTH, HYENA_FILTER_HIDDEN, 2 * HYENA_WIDTH), HYENA_FILTER_HIDDEN ** -0.5),
        'hy_b4': nrm((DEPTH, 2 * HYENA_WIDTH), 0.01),
        'hy_freq': 1.0 + nrm((DEPTH, 3, HYENA_FILTER_HIDDEN), 0.01),
        'hy_dbias': nrm((DEPTH, HYENA_WIDTH), 1.0),
        'attn_sink': nrm((DEPTH, N_HEADS), 1.0),
        'w_br_attn': nrm((DEPTH, ATTN_WIDTH, d), ATTN_WIDTH ** -0.5 * DEEPNORM_BETA),
        'w_br_hyena': nrm((DEPTH, HYENA_WIDTH, d), HYENA_WIDTH ** -0.5 * DEEPNORM_BETA),
        'w_br_fnet': nrm((DEPTH, FNET_WIDTH, d), FNET_WIDTH ** -0.5 * DEEPNORM_BETA),
        'w_out': nrm((DEPTH, d, d), d ** -0.5 * DEEPNORM_BETA),
        'ln1_g': 1.0 + nrm((DEPTH, d), 0.01),
        'ln1_b': nrm((DEPTH, d), 0.01),
        'ln2_g': 1.0 + nrm((DEPTH, d), 0.01),
        'ln2_b': nrm((DEPTH, d), 0.01),
        'rg_w': nrm((DEPTH, d, N_GROUPS), d ** -0.5),
        'rg_b': nrm((DEPTH, N_GROUPS), 0.01),
        're_w': nrm((DEPTH, d, N_EXPERTS), d ** -0.5),
        're_b': nrm((DEPTH, N_EXPERTS), 0.01),
        'moe_w1': nrm((DEPTH, N_EXPERTS, d, EXPERT_HIDDEN), d ** -0.5),
        'moe_w3': nrm((DEPTH, N_EXPERTS, d, EXPERT_HIDDEN), d ** -0.5),
        'moe_w2': nrm((DEPTH, N_EXPERTS, EXPERT_HIDDEN, d), EXPERT_HIDDEN ** -0.5 * DEEPNORM_BETA),
    }


def reference(x, c, ctx, c_ctx, w_ada, b_ada, w_in, conv_w, conv_b, hy_w1, hy_b1, hy_w2, hy_b2,
              hy_w3, hy_b3, hy_w4, hy_b4, hy_freq, hy_dbias, attn_sink, w_br_attn, w_br_hyena,
              w_br_fnet, w_out, ln1_g, ln1_b, ln2_g, ln2_b, rg_w, rg_b, re_w, re_b,
              moe_w1, moe_w3, moe_w2):
    b_, s_, d = x.shape
    n_ctx = ctx.shape[1]
    ang_row, ang_col = axial_rope_angles(s_)
    xc = ctx
    for l in range(DEPTH):
        last = l == DEPTH - 1
        mod = (jax.nn.silu(c) @ w_ada[l] + b_ada[l])[:, None, :]
        mod_c = jax.nn.silu(c_ctx) @ w_ada[l] + b_ada[l]
        sh1, sc1, g1, sh2, sc2, g2 = jnp.split(mod, 6, axis=-1)
        csh1, csc1, cg1, csh2, csc2, cg2 = jnp.split(mod_c, 6, axis=-1)
        filt = (hy_w1[l], hy_b1[l], hy_w2[l], hy_b2[l], hy_w3[l], hy_b3[l], hy_w4[l], hy_b4[l], hy_freq[l])

        h = modulate(x, sh1, sc1)
        hc = modulate(xc, csh1, csc1)
        p = h @ w_in[l]
        if last:
            pc_kv = hc @ w_in[l][:, K_OFF:HY_OFF]
        else:
            pc = hc @ w_in[l]
            pc_kv = pc[..., K_OFF:HY_OFF]
        kc = pc_kv[..., :KV_WIDTH].reshape(b_, n_ctx, N_KV_HEADS, HEAD_DIM)
        vc = pc_kv[..., KV_WIDTH:].reshape(b_, n_ctx, N_KV_HEADS, HEAD_DIM)

        q = apply_axial_rope(p[..., Q_OFF:K_OFF].reshape(b_, s_, N_HEADS, HEAD_DIM), ang_row, ang_col)
        k = apply_axial_rope(p[..., K_OFF:V_OFF].reshape(b_, s_, N_KV_HEADS, HEAD_DIM), ang_row, ang_col)
        v = p[..., V_OFF:HY_OFF].reshape(b_, s_, N_KV_HEADS, HEAD_DIM)
        a_lat = latent_window_attention(q, k, v, kc, vc, attn_sink[l])
        y_lat = mixer_output(p, a_lat, conv_w[l], conv_b[l], filt, hy_dbias[l],
                             w_br_attn[l], w_br_hyena[l], w_br_fnet[l], w_out[l])
        x = post_ln(DEEPNORM_ALPHA * x + g1 * y_lat, ln1_g[l], ln1_b[l])
        if not last:
            qc = pc[..., Q_OFF:K_OFF].reshape(b_, n_ctx, N_HEADS, HEAD_DIM)
            a_ctx = context_attention(qc, kc, vc, attn_sink[l])
            y_ctx = mixer_output(pc, a_ctx, conv_w[l], conv_b[l], filt, hy_dbias[l],
                                 w_br_attn[l], w_br_hyena[l], w_br_fnet[l], w_out[l])
            xc = post_ln(DEEPNORM_ALPHA * xc + cg1 * y_ctx, ln1_g[l], ln1_b[l])

        t_lat = modulate(x, sh2, sc2).reshape(b_ * s_, d)
        if last:
            tokens = t_lat
        else:
            tokens = jnp.concatenate([t_lat, modulate(xc, csh2, csc2).reshape(b_ * n_ctx, d)], 0)
        expert_id, weights = hier_route(tokens, rg_w[l], rg_b[l], re_w[l], re_b[l])
        m = moe_forward(tokens, expert_id, weights, moe_w1[l], moe_w3[l], moe_w2[l])
        x = post_ln(DEEPNORM_ALPHA * x + g2 * m[:b_ * s_].reshape(b_, s_, d), ln2_g[l], ln2_b[l])
        if not last:
            xc = post_ln(DEEPNORM_ALPHA * xc + cg2 * m[b_ * s_:].reshape(b_, n_ctx, d), ln2_g[l], ln2_b[l])
    return x
```

```python
import functools
import math

import numpy as np
import jax
import jax.numpy as jnp
from jax import lax
from jax.experimental import pallas as pl
from jax.experimental.pallas import tpu as pltpu

D_MODEL = 2048
BATCH = 4
SEQ = 4096
DEPTH = 4
GRID_W = 64
CTX_LEN = 256

N_HEADS = 8
N_KV_HEADS = 2
HEAD_DIM = 128
Q_GROUP = N_HEADS // N_KV_HEADS
ATTN_WIDTH = N_HEADS * HEAD_DIM
KV_WIDTH = N_KV_HEADS * HEAD_DIM
WINDOW = 128
ROPE_THETA = 10000.0
MASK_VALUE = -1e30

HYENA_WIDTH = 512
HYENA_SHORT = 3
HYENA_BANDS = 16
HYENA_EMB = 1 + 2 * HYENA_BANDS
HYENA_FILTER_HIDDEN = 64
HYENA_TARGET = 1e-2
HYENA_MAX_DECAY = math.log(HYENA_TARGET) / 0.3
HYENA_MIN_DECAY = math.log(HYENA_TARGET) / 1.5

FNET_WIDTH = 512
FNET_GROUPS = 4
FNET_GROUP_DIM = FNET_WIDTH // FNET_GROUPS

Q_OFF = 0
K_OFF = Q_OFF + ATTN_WIDTH
V_OFF = K_OFF + KV_WIDTH
HY_OFF = V_OFF + KV_WIDTH
FN_OFF = HY_OFF + 3 * HYENA_WIDTH
GATE_OFF = FN_OFF + FNET_WIDTH
IN_WIDTH = GATE_OFF + 3 * D_MODEL
QKV_WIDTH = HY_OFF

N_GROUPS = 4
EXPERTS_PER_GROUP = 8
N_EXPERTS = N_GROUPS * EXPERTS_PER_GROUP
TOP_K = 2
EXPERT_HIDDEN = 512

LN_EPS = 1e-6
DEEPNORM_ALPHA = (2 * DEPTH) ** 0.25

N_LAT = BATCH * SEQ
N_CTX = BATCH * CTX_LEN
NT = N_LAT + N_CTX
MOD_ROWS = 8

LANE = 128
VMEM_LIMIT = 56 * 1024 * 1024

MOE_TB = 256
MOE_BLOCKS = -(-(NT * TOP_K + N_EXPERTS * (MOE_TB - 1)) // MOE_TB)
MOE_SLOTS = MOE_BLOCKS * MOE_TB


def _cparams(n_axes):
    return pltpu.CompilerParams(dimension_semantics=("arbitrary",) * n_axes,
                                vmem_limit_bytes=VMEM_LIMIT)


def _mod_row(i, tm):
    return jnp.minimum((i * tm) // SEQ, BATCH)


def _mod_spec(l, chunk, tm, axis=0):
    def imap(*g):
        return (l, _mod_row(g[axis], tm), 0, chunk)
    return pl.BlockSpec((None, None, 1, D_MODEL), imap)


def _ln(x):
    mu = jnp.mean(x, -1, keepdims=True)
    xc = x - mu
    var = jnp.mean(xc * xc, -1, keepdims=True)
    return xc * lax.rsqrt(var + LN_EPS)


def _ada_kernel(c_ref, w_ref, b_ref, o_ref):
    c = c_ref[...]
    a = (c * jax.nn.sigmoid(c)).astype(jnp.bfloat16)
    o_ref[...] = jnp.dot(a, w_ref[...].astype(jnp.bfloat16),
                         preferred_element_type=jnp.float32) + b_ref[...]


def ada_call(c8, w_ada, b_ada):
    tn = 1024
    n = 6 * D_MODEL
    return pl.pallas_call(
        _ada_kernel,
        out_shape=jax.ShapeDtypeStruct((DEPTH, MOD_ROWS, n), jnp.float32),
        grid=(DEPTH, n // tn),
        in_specs=[pl.BlockSpec((MOD_ROWS, D_MODEL), lambda l, j: (0, 0)),
                  pl.BlockSpec((None, D_MODEL, tn), lambda l, j: (l, 0, j)),
                  pl.BlockSpec((None, 1, tn), lambda l, j: (l, 0, j))],
        out_specs=pl.BlockSpec((None, MOD_ROWS, tn), lambda l, j: (l, 0, j)),
        compiler_params=_cparams(2),
        name="adaln",
    )(c8, w_ada, b_ada.reshape(DEPTH, 1, n))


def _pre_kernel(x_ref, sh_ref, sc_ref, h_ref):
    h_ref[...] = (_ln(x_ref[...]) * (1.0 + sc_ref[...]) + sh_ref[...]).astype(h_ref.dtype)


def pre_call(x, mod4):
    tm = 512
    return pl.pallas_call(
        _pre_kernel,
        out_shape=jax.ShapeDtypeStruct((NT, D_MODEL), jnp.bfloat16),
        grid=(NT // tm,),
        in_specs=[pl.BlockSpec((tm, D_MODEL), lambda i: (i, 0)),
                  _mod_spec(0, 0, tm), _mod_spec(0, 1, tm)],
        out_specs=pl.BlockSpec((tm, D_MODEL), lambda i: (i, 0)),
        compiler_params=_cparams(1),
        name="pre_modulate",
    )(x, mod4, mod4)


def _proj_kernel(x_ref, w_ref, o_ref, wb_ref, *, act):
    @pl.when(pl.program_id(1) == 0)
    def _():
        wb_ref[...] = w_ref[...].astype(jnp.bfloat16)

    acc = jnp.dot(x_ref[...], wb_ref[...], preferred_element_type=jnp.float32)
    if act == "sigmoid":
        acc = jax.nn.sigmoid(acc)
    o_ref[...] = acc.astype(o_ref.dtype)


def proj_call(h, w_in, l, col_off, width, act, out_dtype, name):
    tm, tn = 1024, 512
    off = col_off // tn
    return pl.pallas_call(
        functools.partial(_proj_kernel, act=act),
        out_shape=jax.ShapeDtypeStruct((NT, width), out_dtype),
        grid=(width // tn, NT // tm),
        in_specs=[pl.BlockSpec((tm, D_MODEL), lambda j, i: (i, 0)),
                  pl.BlockSpec((None, D_MODEL, tn), lambda j, i: (l, 0, off + j))],
        out_specs=pl.BlockSpec((tm, tn), lambda j, i: (i, j)),
        scratch_shapes=[pltpu.VMEM((D_MODEL, tn), jnp.bfloat16)],
        compiler_params=_cparams(2),
        name=name,
    )(h, w_in)


def _swap32(x):
    lane = lax.broadcasted_iota(jnp.int32, x.shape, 1)
    up = pltpu.roll(x, LANE - 32, 1)
    dn = pltpu.roll(x, 32, 1)
    return jnp.where((lane & 63) < 32, up, dn)


def _qkv_kernel(x_ref, w_ref, cos_ref, sin_ref, o_ref, wb_ref):
    @pl.when(pl.program_id(0) == 0)
    def _():
        wb_ref[...] = w_ref[...].astype(jnp.bfloat16)

    cos = cos_ref[...]
    sin = sin_ref[...]
    scale = HEAD_DIM ** -0.5
    for hd in range(QKV_WIDTH // HEAD_DIM):
        sl = slice(hd * HEAD_DIM, (hd + 1) * HEAD_DIM)
        acc = jnp.dot(x_ref[...], wb_ref[:, sl], preferred_element_type=jnp.float32)
        if hd < N_HEADS + N_KV_HEADS:
            acc = acc * cos + _swap32(acc) * sin
        if hd < N_HEADS:
            acc = acc * scale
        o_ref[:, sl] = acc.astype(o_ref.dtype)


def qkv_call(h, w_in, l, cos_t, sin_t):
    tm = 512
    n_pos = SEQ // tm

    def rope_map(i):
        return (jnp.where(i * tm < N_LAT, i % n_pos, n_pos), 0)

    return pl.pallas_call(
        _qkv_kernel,
        out_shape=jax.ShapeDtypeStruct((NT, QKV_WIDTH), jnp.bfloat16),
        grid=(NT // tm,),
        in_specs=[pl.BlockSpec((tm, D_MODEL), lambda i: (i, 0)),
                  pl.BlockSpec((None, D_MODEL, QKV_WIDTH), lambda i: (l, 0, 0),
                               pipeline_mode=pl.Buffered(1)),
                  pl.BlockSpec((tm, HEAD_DIM), rope_map),
                  pl.BlockSpec((tm, HEAD_DIM), rope_map)],
        out_specs=pl.BlockSpec((tm, QKV_WIDTH), lambda i: (i, 0)),
        scratch_shapes=[pltpu.VMEM((D_MODEL, QKV_WIDTH), jnp.bfloat16)],
        compiler_params=_cparams(1),
        name="qkv_rope",
    )(h, w_in, cos_t, sin_t)


def rope_tables(tm):
    n_freq = HEAD_DIM // 4
    inv_freq = ROPE_THETA ** (-np.arange(n_freq, dtype=np.float64) / n_freq)
    t = np.arange(SEQ)
    ang_row = (t // GRID_W)[:, None] * inv_freq[None, :]
    ang_col = (t % GRID_W)[:, None] * inv_freq[None, :]
    cos = np.concatenate([np.cos(ang_row), np.cos(ang_row), np.cos(ang_col), np.cos(ang_col)], 1)
    sin = np.concatenate([-np.sin(ang_row), np.sin(ang_row), -np.sin(ang_col), np.sin(ang_col)], 1)
    cos = np.concatenate([cos, np.ones((tm, HEAD_DIM))], 0)
    sin = np.concatenate([sin, np.zeros((tm, HEAD_DIM))], 0)
    return jnp.asarray(cos, jnp.float32), jnp.asarray(sin, jnp.float32)


ATT_QT = 256
ATT_LAT_TILES = N_LAT // ATT_QT
ATT_TILES_PER_SEQ = SEQ // ATT_QT


def _attn_kernel(sink_ref, q_ref, kp_ref, kc_ref, kn_ref, vp_ref, vc_ref, vn_ref,
                 kx_ref, vx_ref, o_ref, *, l):
    i = pl.program_id(0)
    h = pl.program_id(1)
    is_lat = i < ATT_LAT_TILES
    base = (i % ATT_TILES_PER_SEQ) * ATT_QT
    n_loc = ATT_QT + 2 * WINDOW
    r = lax.broadcasted_iota(jnp.int32, (ATT_QT, n_loc), 0)
    j = lax.broadcasted_iota(jnp.int32, (ATT_QT, n_loc), 1)
    kpos = base - WINDOW + j
    dist = j - r
    k_end = jnp.where(is_lat, SEQ, 0)
    valid = (dist >= 0) & (dist <= 2 * WINDOW) & (kpos >= 0) & (kpos < k_end)

    k_loc = jnp.concatenate([kp_ref[...], kc_ref[...], kn_ref[...]], axis=0)
    v_loc = jnp.concatenate([vp_ref[...], vc_ref[...], vn_ref[...]], axis=0)
    k_ctx = kx_ref[...]
    v_ctx = vx_ref[...]
    dn = (((1,), (1,)), ((), ()))
    for g in range(Q_GROUP):
        sl = slice(g * HEAD_DIM, (g + 1) * HEAD_DIM)
        q = q_ref[:, sl]
        s_loc = lax.dot_general(q, k_loc, dn, preferred_element_type=jnp.float32)
        s_loc = jnp.where(valid, s_loc, MASK_VALUE)
        s_ctx = lax.dot_general(q, k_ctx, dn, preferred_element_type=jnp.float32)
        sink = sink_ref[l, h * Q_GROUP + g]
        m = jnp.maximum(jnp.maximum(jnp.max(s_loc, -1, keepdims=True),
                                    jnp.max(s_ctx, -1, keepdims=True)), sink)
        e_loc = jnp.exp(s_loc - m)
        e_ctx = jnp.exp(s_ctx - m)
        den = (jnp.sum(e_loc, -1, keepdims=True) + jnp.sum(e_ctx, -1, keepdims=True)
               + jnp.exp(sink - m))
        o = (jnp.dot(e_loc.astype(jnp.bfloat16), v_loc, preferred_element_type=jnp.float32)
             + jnp.dot(e_ctx.astype(jnp.bfloat16), v_ctx, preferred_element_type=jnp.float32))
        o_ref[:, sl] = (o / den).astype(o_ref.dtype)


def attn_call(qkv, sink, l):
    blk = WINDOW
    per = ATT_QT // blk
    last_blk = NT // blk - 1
    kcol, vcol = K_OFF // HEAD_DIM, V_OFF // HEAD_DIM

    def batch_of(i):
        return jnp.where(i < ATT_LAT_TILES, i // ATT_TILES_PER_SEQ, i - ATT_LAT_TILES)

    def prev_map(col):
        return lambda i, h: (jnp.maximum(i * per - 1, 0), col + h)

    def next_map(col):
        return lambda i, h: (jnp.minimum(i * per + per, last_blk), col + h)

    def cur_map(col):
        return lambda i, h: (i, col + h)

    def ctx_map(col):
        return lambda i, h: (N_LAT // CTX_LEN + batch_of(i), col + h)

    return pl.pallas_call(
        functools.partial(_attn_kernel, l=l),
        out_shape=jax.ShapeDtypeStruct((NT, ATTN_WIDTH), jnp.bfloat16),
        grid=(NT // ATT_QT, N_KV_HEADS),
        in_specs=[pl.BlockSpec(memory_space=pltpu.SMEM),
                  pl.BlockSpec((ATT_QT, Q_GROUP * HEAD_DIM), lambda i, h: (i, h)),
                  pl.BlockSpec((blk, HEAD_DIM), prev_map(kcol)),
                  pl.BlockSpec((ATT_QT, HEAD_DIM), cur_map(kcol)),
                  pl.BlockSpec((blk, HEAD_DIM), next_map(kcol)),
                  pl.BlockSpec((blk, HEAD_DIM), prev_map(vcol)),
                  pl.BlockSpec((ATT_QT, HEAD_DIM), cur_map(vcol)),
                  pl.BlockSpec((blk, HEAD_DIM), next_map(vcol)),
                  pl.BlockSpec((CTX_LEN, HEAD_DIM), ctx_map(kcol)),
                  pl.BlockSpec((CTX_LEN, HEAD_DIM), ctx_map(vcol))],
        out_specs=pl.BlockSpec((ATT_QT, Q_GROUP * HEAD_DIM), lambda i, h: (i, h)),
        compiler_params=_cparams(2),
        name="window_attention",
    )(sink, qkv, qkv, qkv, qkv, qkv, qkv, qkv, qkv, qkv)


def _mix_kernel(a_ref, yh_ref, yf_ref, g_ref, x_ref, wa_ref, wh_ref, wf_ref, wo_ref,
                g1_ref, sh2_ref, sc2_ref, lg_ref, lb_ref, xo_ref, t_ref, y_ref):
    tn = 512
    for c in range(D_MODEL // tn):
        sl = slice(c * tn, (c + 1) * tn)
        ya = jnp.dot(a_ref[...], wa_ref[:, sl], preferred_element_type=jnp.float32)
        yh = jnp.dot(yh_ref[...], wh_ref[:, sl], preferred_element_type=jnp.float32)
        yf = jnp.dot(yf_ref[...], wf_ref[:, sl], preferred_element_type=jnp.float32)
        y = (g_ref[:, sl].astype(jnp.float32) * ya
             + g_ref[:, D_MODEL + c * tn:D_MODEL + (c + 1) * tn].astype(jnp.float32) * yh
             + g_ref[:, 2 * D_MODEL + c * tn:2 * D_MODEL + (c + 1) * tn].astype(jnp.float32) * yf)
        y_ref[:, sl] = y.astype(y_ref.dtype)
    o = jnp.dot(y_ref[...], wo_ref[...], preferred_element_type=jnp.float32)
    xn = _ln(DEEPNORM_ALPHA * x_ref[...] + g1_ref[...] * o) * lg_ref[...] + lb_ref[...]
    xo_ref[...] = xn
    t_ref[...] = (_ln(xn) * (1.0 + sc2_ref[...]) + sh2_ref[...]).astype(t_ref.dtype)


def mix_call(a, yh, yf, gates, x, wa, wh, wf, wo, mod4, ln_g, ln_b, l):
    tm = 256
    row = lambda i: (i, 0)
    full = lambda i: (0, 0)
    const = pl.Buffered(1)
    return pl.pallas_call(
        _mix_kernel,
        out_shape=(jax.ShapeDtypeStruct((NT, D_MODEL), jnp.float32),
                   jax.ShapeDtypeStruct((NT, D_MODEL), jnp.bfloat16)),
        grid=(NT // tm,),
        in_specs=[pl.BlockSpec((tm, ATTN_WIDTH), row),
                  pl.BlockSpec((tm, HYENA_WIDTH), row),
                  pl.BlockSpec((tm, FNET_WIDTH), row),
                  pl.BlockSpec((tm, 3 * D_MODEL), row),
                  pl.BlockSpec((tm, D_MODEL), row),
                  pl.BlockSpec((ATTN_WIDTH, D_MODEL), full, pipeline_mode=const),
                  pl.BlockSpec((HYENA_WIDTH, D_MODEL), full, pipeline_mode=const),
                  pl.BlockSpec((FNET_WIDTH, D_MODEL), full, pipeline_mode=const),
                  pl.BlockSpec((D_MODEL, D_MODEL), full, pipeline_mode=const),
                  _mod_spec(l, 2, tm), _mod_spec(l, 3, tm), _mod_spec(l, 4, tm),
                  pl.BlockSpec((None, 1, D_MODEL), lambda i: (l, 0, 0)),
                  pl.BlockSpec((None, 1, D_MODEL), lambda i: (l, 0, 0))],
        out_specs=(pl.BlockSpec((tm, D_MODEL), row), pl.BlockSpec((tm, D_MODEL), row)),
        scratch_shapes=[pltpu.VMEM((tm, D_MODEL), jnp.bfloat16)],
        compiler_params=_cparams(1),
        name="mixer_out",
    )(a, yh, yf, gates, x, wa, wh, wf, wo, mod4, mod4, mod4, ln_g, ln_b)


ROUTE_LANES = LANE


def _route_kernel(t_ref, w_ref, b_ref, o_ref):
    logits = jnp.dot(t_ref[...], w_ref[...], preferred_element_type=jnp.float32) + b_ref[...]
    lane = lax.broadcasted_iota(jnp.int32, logits.shape, 1)
    neg = -jnp.inf
    big = jnp.int32(2 ** 30)

    def first_argmax(vals):
        mx = jnp.max(vals, -1, keepdims=True)
        idx = jnp.min(jnp.where(vals == mx, lane, big), -1, keepdims=True)
        return mx, idx

    g_vals = jnp.where(lane < N_GROUPS, logits, neg)
    g_max, grp = first_argmax(g_vals)
    g_w = 1.0 / jnp.sum(jnp.exp(g_vals - g_max), -1, keepdims=True)
    lo = N_GROUPS + grp * EXPERTS_PER_GROUP
    e_vals = jnp.where((lane >= lo) & (lane < lo + EXPERTS_PER_GROUP), logits, neg)
    v1, i1 = first_argmax(e_vals)
    v2, i2 = first_argmax(jnp.where(lane == i1, neg, e_vals))
    e2 = jnp.exp(v2 - v1)
    w1 = g_w / (1.0 + e2)
    w2 = g_w * e2 / (1.0 + e2)
    id1 = (i1 - N_GROUPS).astype(jnp.float32)
    id2 = (i2 - N_GROUPS).astype(jnp.float32)
    o_ref[...] = jnp.where(lane == 0, id1, jnp.where(lane == 1, id2,
                           jnp.where(lane == 2, w1, jnp.where(lane == 3, w2, 0.0))))


def route_call(t, wr, br):
    tm = 512
    return pl.pallas_call(
        _route_kernel,
        out_shape=jax.ShapeDtypeStruct((NT, ROUTE_LANES), jnp.float32),
        grid=(NT // tm,),
        in_specs=[pl.BlockSpec((tm, D_MODEL), lambda i: (i, 0)),
                  pl.BlockSpec((D_MODEL, ROUTE_LANES), lambda i: (0, 0)),
                  pl.BlockSpec((1, ROUTE_LANES), lambda i: (0, 0))],
        out_specs=pl.BlockSpec((tm, ROUTE_LANES), lambda i: (i, 0)),
        compiler_params=_cparams(1),
        name="moe_router",
    )(t, wr, br)


def _moe_kernel(be_ref, nu_ref, x_ref, w1_ref, w3_ref, w2_ref, o_ref, b1_ref, b3_ref, b2_ref):
    i = pl.program_id(0)
    first = (i == 0) | (be_ref[i] != be_ref[jnp.maximum(i - 1, 0)])

    @pl.when(first)
    def _():
        b1_ref[...] = w1_ref[...].astype(jnp.bfloat16)
        b3_ref[...] = w3_ref[...].astype(jnp.bfloat16)
        b2_ref[...] = w2_ref[...].astype(jnp.bfloat16)

    @pl.when(i < nu_ref[0])
    def _():
        x = x_ref[...]
        h1 = jnp.dot(x, b1_ref[...], preferred_element_type=jnp.float32)
        h3 = jnp.dot(x, b3_ref[...], preferred_element_type=jnp.float32)
        hh = (h1 * jax.nn.sigmoid(h1) * h3).astype(jnp.bfloat16)
        o_ref[...] = jnp.dot(hh, b2_ref[...], preferred_element_type=jnp.float32)

    @pl.when(i >= nu_ref[0])
    def _():
        o_ref[...] = jnp.zeros_like(o_ref)


def moe_call(block_e, n_used, xb, w1, w3, w2, l):
    grid_spec = pltpu.PrefetchScalarGridSpec(
        num_scalar_prefetch=2,
        grid=(MOE_BLOCKS,),
        in_specs=[pl.BlockSpec((MOE_TB, D_MODEL), lambda i, be, nu: (i, 0)),
                  pl.BlockSpec((None, None, D_MODEL, EXPERT_HIDDEN), lambda i, be, nu: (l, be[i], 0, 0)),
                  pl.BlockSpec((None, None, D_MODEL, EXPERT_HIDDEN), lambda i, be, nu: (l, be[i], 0, 0)),
                  pl.BlockSpec((None, None, EXPERT_HIDDEN, D_MODEL), lambda i, be, nu: (l, be[i], 0, 0))],
        out_specs=pl.BlockSpec((MOE_TB, D_MODEL), lambda i, be, nu: (i, 0)),
        scratch_shapes=[pltpu.VMEM((D_MODEL, EXPERT_HIDDEN), jnp.bfloat16),
                        pltpu.VMEM((D_MODEL, EXPERT_HIDDEN), jnp.bfloat16),
                        pltpu.VMEM((EXPERT_HIDDEN, D_MODEL), jnp.bfloat16)])
    return pl.pallas_call(
        _moe_kernel,
        out_shape=jax.ShapeDtypeStruct((MOE_SLOTS, D_MODEL), jnp.float32),
        grid_spec=grid_spec,
        compiler_params=_cparams(1),
        name="moe_experts",
    )(block_e, n_used, xb, w1, w3, w2)


def _post_kernel(x_ref, y0_ref, y1_ref, r_ref, g2_ref, lg_ref, lb_ref, sh_ref, sc_ref,
                 xo_ref, h_ref):
    r = r_ref[...]
    m = r[:, 2:3] * y0_ref[...] + r[:, 3:4] * y1_ref[...]
    xn = _ln(DEEPNORM_ALPHA * x_ref[...] + g2_ref[...] * m) * lg_ref[...] + lb_ref[...]
    xo_ref[...] = xn
    h_ref[...] = (_ln(xn) * (1.0 + sc_ref[...]) + sh_ref[...]).astype(h_ref.dtype)


def post_call(x, y0, y1, route, mod4, ln_g, ln_b, l):
    tm = 512
    ln = min(l + 1, DEPTH - 1)
    row = lambda i: (i, 0)
    return pl.pallas_call(
        _post_kernel,
        out_shape=(jax.ShapeDtypeStruct((NT, D_MODEL), jnp.float32),
                   jax.ShapeDtypeStruct((NT, D_MODEL), jnp.bfloat16)),
        grid=(NT // tm,),
        in_specs=[pl.BlockSpec((tm, D_MODEL), row),
                  pl.BlockSpec((tm, D_MODEL), row),
                  pl.BlockSpec((tm, D_MODEL), row),
                  pl.BlockSpec((tm, ROUTE_LANES), row),
                  _mod_spec(l, 5, tm),
                  pl.BlockSpec((None, 1, D_MODEL), lambda i: (l, 0, 0)),
                  pl.BlockSpec((None, 1, D_MODEL), lambda i: (l, 0, 0)),
                  _mod_spec(ln, 0, tm), _mod_spec(ln, 1, tm)],
        out_specs=(pl.BlockSpec((tm, D_MODEL), row), pl.BlockSpec((tm, D_MODEL), row)),
        compiler_params=_cparams(1),
        name="moe_combine_postln",
    )(x, y0, y1, route, mod4, ln_g, ln_b, mod4, mod4)


def _short_conv(u, w, b):
    n = u.shape[1]
    up = jnp.pad(u, ((0, 0), (1, 1), (0, 0)))
    out = b
    for j in range(HYENA_SHORT):
        out = out + up[:, j:j + n] * w[j]
    return out


def _hyena_filter(n, w1, b1, w2, b2, w3, b3, w4, b4, freq):
    f32 = jnp.float32
    hp = lax.Precision.HIGHEST
    t = jnp.linspace(0.0, 1.0, n, dtype=f32)[:, None]
    w = 2.0 * math.pi * jnp.arange(n, dtype=f32)[:, None] / n
    bands = jnp.linspace(1e-4, HYENA_BANDS - 1, HYENA_BANDS, dtype=f32)[None, :]
    z = jnp.concatenate([t, jnp.cos(bands * w), -jnp.sin(bands * w)], -1)
    hdn = jnp.sin(freq[0] * (jnp.dot(z, w1, precision=hp) + b1))
    hdn = jnp.sin(freq[1] * (jnp.dot(hdn, w2, precision=hp) + b2))
    hdn = jnp.sin(freq[2] * (jnp.dot(hdn, w3, precision=hp) + b3))
    filt = (jnp.dot(hdn, w4, precision=hp) + b4).reshape(n, 2, HYENA_WIDTH)
    deltas = jnp.abs(jnp.linspace(HYENA_MIN_DECAY, HYENA_MAX_DECAY, HYENA_WIDTH, dtype=f32))
    filt = filt * jnp.exp(-t * deltas[None, :])[:, None, :]
    circ = jnp.concatenate([filt[:, 0], jnp.zeros((1, HYENA_WIDTH), f32), filt[:0:-1, 1]], 0)
    return circ / (jnp.sum(jnp.abs(circ), 0, keepdims=True) + 1e-6)


def _hyena(u, conv_w, conv_b, filt_params, d_bias):
    n = u.shape[1]
    uc = _short_conv(u, conv_w, conv_b)
    x0, x1, v = jnp.split(uc, 3, axis=-1)
    circ = _hyena_filter(n, *filt_params)
    z = x1 * v
    zf = jnp.fft.rfft(z, n=2 * n, axis=1)
    hf = jnp.fft.rfft(circ, n=2 * n, axis=0)
    y = jnp.fft.irfft(zf * hf[None], n=2 * n, axis=1)[:, :n] + z * d_bias
    return x0 * y


def _fnet(u):
    b_, n, _ = u.shape
    ug = u.reshape(b_, n, FNET_GROUPS, FNET_GROUP_DIM)
    y = jnp.fft.fftn(ug, axes=(1, 3), norm='ortho').real
    return y.reshape(b_, n, FNET_WIDTH)


def _both(fn, arr, width):
    lat = fn(arr[:N_LAT].reshape(BATCH, SEQ, width)).reshape(N_LAT, -1)
    ctx = fn(arr[N_LAT:].reshape(BATCH, CTX_LEN, width)).reshape(N_CTX, -1)
    return jnp.concatenate([lat, ctx], 0)


def _dispatch(route):
    e_flat = route[:, :TOP_K].astype(jnp.int32).reshape(-1)
    onehot = (e_flat[:, None] == jnp.arange(N_EXPERTS, dtype=jnp.int32)[None, :]).astype(jnp.int32)
    csum = jnp.cumsum(onehot, 0)
    rank = jnp.sum((csum - onehot) * onehot, 1)
    counts = csum[-1]
    padded = ((counts + MOE_TB - 1) // MOE_TB) * MOE_TB
    pend = jnp.cumsum(padded)
    pstart = pend - padded
    dest = pstart[e_flat] + rank
    tok = jnp.arange(NT * TOP_K, dtype=jnp.int32) // TOP_K
    slot_tok = jnp.zeros((MOE_SLOTS,), jnp.int32).at[dest].set(tok)
    block_e = jnp.minimum(jnp.searchsorted(pend, jnp.arange(MOE_BLOCKS, dtype=jnp.int32) * MOE_TB,
                                           side='right'), N_EXPERTS - 1).astype(jnp.int32)
    n_used = (pend[-1] // MOE_TB).astype(jnp.int32).reshape(1)
    return slot_tok, dest.reshape(NT, TOP_K), block_e, n_used


def kernel(x, c, ctx, c_ctx, w_ada, b_ada, w_in, conv_w, conv_b, hy_w1, hy_b1, hy_w2, hy_b2, hy_w3, hy_b3, hy_w4, hy_b4, hy_freq, hy_dbias, attn_sink, w_br_attn, w_br_hyena, w_br_fnet, w_out, ln1_g, ln1_b, ln2_g, ln2_b, rg_w, rg_b, re_w, re_b, moe_w1, moe_w3, moe_w2):
    bf16 = jnp.bfloat16
    xs = jnp.concatenate([x.reshape(N_LAT, D_MODEL), ctx.reshape(N_CTX, D_MODEL)], 0)
    c8 = jnp.concatenate([c, c_ctx[None, :], jnp.zeros((MOD_ROWS - BATCH - 1, D_MODEL), c.dtype)], 0)
    mod4 = ada_call(c8, w_ada, b_ada).reshape(DEPTH, MOD_ROWS, 1, 6 * D_MODEL)
    cos_t, sin_t = rope_tables(512)
    ln1_g3, ln1_b3 = ln1_g.reshape(DEPTH, 1, D_MODEL), ln1_b.reshape(DEPTH, 1, D_MODEL)
    ln2_g3, ln2_b3 = ln2_g.reshape(DEPTH, 1, D_MODEL), ln2_b.reshape(DEPTH, 1, D_MODEL)
    pad = jnp.zeros((DEPTH, D_MODEL, ROUTE_LANES - N_GROUPS - N_EXPERTS), jnp.float32)
    w_route = jnp.concatenate([rg_w, re_w, pad], -1).astype(bf16)
    b_route = jnp.concatenate([rg_b, re_b, pad[:, 0]], -1).reshape(DEPTH, 1, ROUTE_LANES)

    h = pre_call(xs, mod4)
    for l in range(DEPTH):
        qkv = qkv_call(h, w_in, l, cos_t, sin_t)
        u_hy = proj_call(h, w_in, l, HY_OFF, 3 * HYENA_WIDTH, None, jnp.float32, "proj_hyena")
        u_fn = proj_call(h, w_in, l, FN_OFF, FNET_WIDTH, None, jnp.float32, "proj_fnet")
        gates = proj_call(h, w_in, l, GATE_OFF, 3 * D_MODEL, "sigmoid", bf16, "proj_gates")
        a = attn_call(qkv, attn_sink, l)
        filt = (hy_w1[l], hy_b1[l], hy_w2[l], hy_b2[l], hy_w3[l], hy_b3[l], hy_w4[l], hy_b4[l], hy_freq[l])
        y_hy = _both(lambda u: _hyena(u, conv_w[l], conv_b[l], filt, hy_dbias[l]), u_hy, 3 * HYENA_WIDTH)
        y_fn = _both(_fnet, u_fn, FNET_WIDTH)
        xs, t = mix_call(a, y_hy.astype(bf16), y_fn.astype(bf16), gates, xs,
                         w_br_attn[l].astype(bf16), w_br_hyena[l].astype(bf16),
                         w_br_fnet[l].astype(bf16), w_out[l].astype(bf16),
                         mod4, ln1_g3, ln1_b3, l)
        route = route_call(t, w_route[l], b_route[l])
        slot_tok, dest, block_e, n_used = _dispatch(route)
        yb = moe_call(block_e, n_used, t[slot_tok], moe_w1, moe_w3, moe_w2, l)
        xs, h = post_call(xs, yb[dest[:, 0]], yb[dest[:, 1]], route, mod4, ln2_g3, ln2_b3, l)
    return xs[:N_LAT].reshape(BATCH, SEQ, D_MODEL)
```

```python
import functools
import math

import numpy as np
import jax
import jax.numpy as jnp
from jax import lax
from jax.experimental import pallas as pl
from jax.experimental.pallas import tpu as pltpu

D_MODEL = 2048
BATCH = 4
SEQ = 4096
DEPTH = 4
GRID_W = 64
CTX_LEN = 256

N_HEADS = 8
N_KV_HEADS = 2
HEAD_DIM = 128
Q_GROUP = N_HEADS // N_KV_HEADS
ATTN_WIDTH = N_HEADS * HEAD_DIM
KV_WIDTH = N_KV_HEADS * HEAD_DIM
WINDOW = 128
ROPE_THETA = 10000.0
MASK_VALUE = -1e30

HYENA_WIDTH = 512
HYENA_SHORT = 3
HYENA_BANDS = 16
HYENA_EMB = 1 + 2 * HYENA_BANDS
HYENA_FILTER_HIDDEN = 64
HYENA_TARGET = 1e-2
HYENA_MAX_DECAY = math.log(HYENA_TARGET) / 0.3
HYENA_MIN_DECAY = math.log(HYENA_TARGET) / 1.5

FNET_WIDTH = 512
FNET_GROUPS = 4
FNET_GROUP_DIM = FNET_WIDTH // FNET_GROUPS

Q_OFF = 0
K_OFF = Q_OFF + ATTN_WIDTH
V_OFF = K_OFF + KV_WIDTH
HY_OFF = V_OFF + KV_WIDTH
FN_OFF = HY_OFF + 3 * HYENA_WIDTH
GATE_OFF = FN_OFF + FNET_WIDTH
IN_WIDTH = GATE_OFF + 3 * D_MODEL
QKV_WIDTH = HY_OFF

N_GROUPS = 4
EXPERTS_PER_GROUP = 8
N_EXPERTS = N_GROUPS * EXPERTS_PER_GROUP
TOP_K = 2
EXPERT_HIDDEN = 512

LN_EPS = 1e-6
DEEPNORM_ALPHA = (2 * DEPTH) ** 0.25

N_LAT = BATCH * SEQ
N_CTX = BATCH * CTX_LEN
NT = N_LAT + N_CTX
MOD_ROWS = 8

LANE = 128
VMEM_LIMIT = 56 * 1024 * 1024

MOE_TB = 256
MOE_BLOCKS = -(-(NT * TOP_K + N_EXPERTS * (MOE_TB - 1)) // MOE_TB)
MOE_SLOTS = MOE_BLOCKS * MOE_TB


def _cparams(n_axes):
    return pltpu.CompilerParams(dimension_semantics=("arbitrary",) * n_axes,
                                vmem_limit_bytes=VMEM_LIMIT)


def _mod_row(i, tm):
    return jnp.minimum((i * tm) // SEQ, BATCH)


def _mod_spec(l, chunk, tm, axis=0):
    def imap(*g):
        return (l, _mod_row(g[axis], tm), 0, chunk)
    return pl.BlockSpec((None, None, 1, D_MODEL), imap)


def _ln(x):
    mu = jnp.mean(x, -1, keepdims=True)
    xc = x - mu
    var = jnp.mean(xc * xc, -1, keepdims=True)
    return xc * lax.rsqrt(var + LN_EPS)


def _ada_kernel(c_ref, w_ref, b_ref, o_ref):
    c = c_ref[...]
    a = (c * jax.nn.sigmoid(c)).astype(jnp.bfloat16)
    o_ref[...] = jnp.dot(a, w_ref[...].astype(jnp.bfloat16),
                         preferred_element_type=jnp.float32) + b_ref[...]


def ada_call(c8, w_ada, b_ada):
    tn = 1024
    n = 6 * D_MODEL
    return pl.pallas_call(
        _ada_kernel,
        out_shape=jax.ShapeDtypeStruct((DEPTH, MOD_ROWS, n), jnp.float32),
        grid=(DEPTH, n // tn),
        in_specs=[pl.BlockSpec((MOD_ROWS, D_MODEL), lambda l, j: (0, 0)),
                  pl.BlockSpec((None, D_MODEL, tn), lambda l, j: (l, 0, j)),
                  pl.BlockSpec((None, 1, tn), lambda l, j: (l, 0, j))],
        out_specs=pl.BlockSpec((None, MOD_ROWS, tn), lambda l, j: (l, 0, j)),
        compiler_params=_cparams(2),
        name="adaln",
    )(c8, w_ada, b_ada.reshape(DEPTH, 1, n))


def _pre_kernel(x_ref, sh_ref, sc_ref, h_ref):
    h_ref[...] = (_ln(x_ref[...]) * (1.0 + sc_ref[...]) + sh_ref[...]).astype(h_ref.dtype)


def pre_call(x, mod4):
    tm = 512
    return pl.pallas_call(
        _pre_kernel,
        out_shape=jax.ShapeDtypeStruct((NT, D_MODEL), jnp.bfloat16),
        grid=(NT // tm,),
        in_specs=[pl.BlockSpec((tm, D_MODEL), lambda i: (i, 0)),
                  _mod_spec(0, 0, tm), _mod_spec(0, 1, tm)],
        out_specs=pl.BlockSpec((tm, D_MODEL), lambda i: (i, 0)),
        compiler_params=_cparams(1),
        name="pre_modulate",
    )(x, mod4, mod4)


def _proj_kernel(x_ref, w_ref, o_ref, wb_ref, *, act):
    @pl.when(pl.program_id(1) == 0)
    def _():
        wb_ref[...] = w_ref[...].astype(jnp.bfloat16)

    acc = jnp.dot(x_ref[...], wb_ref[...], preferred_element_type=jnp.float32)
    if act == "sigmoid":
        acc = jax.nn.sigmoid(acc)
    o_ref[...] = acc.astype(o_ref.dtype)


def proj_call(h, w_in, l, col_off, width, act, out_dtype, name):
    tm, tn = 1024, 512
    off = col_off // tn
    return pl.pallas_call(
        functools.partial(_proj_kernel, act=act),
        out_shape=jax.ShapeDtypeStruct((NT, width), out_dtype),
        grid=(width // tn, NT // tm),
        in_specs=[pl.BlockSpec((tm, D_MODEL), lambda j, i: (i, 0)),
                  pl.BlockSpec((None, D_MODEL, tn), lambda j, i: (l, 0, off + j))],
        out_specs=pl.BlockSpec((tm, tn), lambda j, i: (i, j)),
        scratch_shapes=[pltpu.VMEM((D_MODEL, tn), jnp.bfloat16)],
        compiler_params=_cparams(2),
        name=name,
    )(h, w_in)


def _swap32(x):
    lane = lax.broadcasted_iota(jnp.int32, x.shape, 1)
    up = pltpu.roll(x, LANE - 32, 1)
    dn = pltpu.roll(x, 32, 1)
    return jnp.where((lane & 63) < 32, up, dn)


def _qkv_kernel(x_ref, w_ref, cos_ref, sin_ref, o_ref, wb_ref):
    @pl.when(pl.program_id(0) == 0)
    def _():
        wb_ref[...] = w_ref[...].astype(jnp.bfloat16)

    cos = cos_ref[...]
    sin = sin_ref[...]
    scale = HEAD_DIM ** -0.5
    for hd in range(QKV_WIDTH // HEAD_DIM):
        sl = slice(hd * HEAD_DIM, (hd + 1) * HEAD_DIM)
        acc = jnp.dot(x_ref[...], wb_ref[:, sl], preferred_element_type=jnp.float32)
        if hd < N_HEADS + N_KV_HEADS:
            acc = acc * cos + _swap32(acc) * sin
        if hd < N_HEADS:
            acc = acc * scale
        o_ref[:, sl] = acc.astype(o_ref.dtype)


def qkv_call(h, w_in, l, cos_t, sin_t):
    tm = 512
    n_pos = SEQ // tm

    def rope_map(i):
        return (jnp.where(i * tm < N_LAT, i % n_pos, n_pos), 0)

    return pl.pallas_call(
        _qkv_kernel,
        out_shape=jax.ShapeDtypeStruct((NT, QKV_WIDTH), jnp.bfloat16),
        grid=(NT // tm,),
        in_specs=[pl.BlockSpec((tm, D_MODEL), lambda i: (i, 0)),
                  pl.BlockSpec((None, D_MODEL, QKV_WIDTH), lambda i: (l, 0, 0),
                               pipeline_mode=pl.Buffered(1)),
                  pl.BlockSpec((tm, HEAD_DIM), rope_map),
                  pl.BlockSpec((tm, HEAD_DIM), rope_map)],
        out_specs=pl.BlockSpec((tm, QKV_WIDTH), lambda i: (i, 0)),
        scratch_shapes=[pltpu.VMEM((D_MODEL, QKV_WIDTH), jnp.bfloat16)],
        compiler_params=_cparams(1),
        name="qkv_rope",
    )(h, w_in, cos_t, sin_t)


def rope_tables(tm):
    n_freq = HEAD_DIM // 4
    inv_freq = ROPE_THETA ** (-np.arange(n_freq, dtype=np.float64) / n_freq)
    t = np.arange(SEQ)
    ang_row = (t // GRID_W)[:, None] * inv_freq[None, :]
    ang_col = (t % GRID_W)[:, None] * inv_freq[None, :]
    cos = np.concatenate([np.cos(ang_row), np.cos(ang_row), np.cos(ang_col), np.cos(ang_col)], 1)
    sin = np.concatenate([-np.sin(ang_row), np.sin(ang_row), -np.sin(ang_col), np.sin(ang_col)], 1)
    cos = np.concatenate([cos, np.ones((tm, HEAD_DIM))], 0)
    sin = np.concatenate([sin, np.zeros((tm, HEAD_DIM))], 0)
    return jnp.asarray(cos, jnp.float32), jnp.asarray(sin, jnp.float32)


ATT_QT = 256
ATT_LAT_TILES = N_LAT // ATT_QT
ATT_TILES_PER_SEQ = SEQ // ATT_QT


def _attn_kernel(sink_ref, q_ref, kp_ref, kc_ref, kn_ref, vp_ref, vc_ref, vn_ref,
                 kx_ref, vx_ref, o_ref, *, l):
    i = pl.program_id(0)
    h = pl.program_id(1)
    is_lat = i < ATT_LAT_TILES
    base = (i % ATT_TILES_PER_SEQ) * ATT_QT
    n_loc = ATT_QT + 2 * WINDOW
    r = lax.broadcasted_iota(jnp.int32, (ATT_QT, n_loc), 0)
    j = lax.broadcasted_iota(jnp.int32, (ATT_QT, n_loc), 1)
    kpos = base - WINDOW + j
    dist = j - r
    k_end = jnp.where(is_lat, SEQ, 0)
    valid = (dist >= 0) & (dist <= 2 * WINDOW) & (kpos >= 0) & (kpos < k_end)

    k_loc = jnp.concatenate([kp_ref[...], kc_ref[...], kn_ref[...]], axis=0)
    v_loc = jnp.concatenate([vp_ref[...], vc_ref[...], vn_ref[...]], axis=0)
    k_ctx = kx_ref[...]
    v_ctx = vx_ref[...]
    dn = (((1,), (1,)), ((), ()))
    for g in range(Q_GROUP):
        sl = slice(g * HEAD_DIM, (g + 1) * HEAD_DIM)
        q = q_ref[:, sl]
        s_loc = lax.dot_general(q, k_loc, dn, preferred_element_type=jnp.float32)
        s_loc = jnp.where(valid, s_loc, MASK_VALUE)
        s_ctx = lax.dot_general(q, k_ctx, dn, preferred_element_type=jnp.float32)
        sink = sink_ref[l, h * Q_GROUP + g]
        m = jnp.maximum(jnp.maximum(jnp.max(s_loc, -1, keepdims=True),
                                    jnp.max(s_ctx, -1, keepdims=True)), sink)
        e_loc = jnp.exp(s_loc - m)
        e_ctx = jnp.exp(s_ctx - m)
        den = (jnp.sum(e_loc, -1, keepdims=True) + jnp.sum(e_ctx, -1, keepdims=True)
               + jnp.exp(sink - m))
        o = (jnp.dot(e_loc.astype(jnp.bfloat16), v_loc, preferred_element_type=jnp.float32)
             + jnp.dot(e_ctx.astype(jnp.bfloat16), v_ctx, preferred_element_type=jnp.float32))
        o_ref[:, sl] = (o / den).astype(o_ref.dtype)


def attn_call(qkv, sink, l):
    blk = WINDOW
    per = ATT_QT // blk
    last_blk = NT // blk - 1
    kcol, vcol = K_OFF // HEAD_DIM, V_OFF // HEAD_DIM

    def batch_of(i):
        return jnp.where(i < ATT_LAT_TILES, i // ATT_TILES_PER_SEQ, i - ATT_LAT_TILES)

    def prev_map(col):
        return lambda i, h: (jnp.maximum(i * per - 1, 0), col + h)

    def next_map(col):
        return lambda i, h: (jnp.minimum(i * per + per, last_blk), col + h)

    def cur_map(col):
        return lambda i, h: (i, col + h)

    def ctx_map(col):
        return lambda i, h: (N_LAT // CTX_LEN + batch_of(i), col + h)

    return pl.pallas_call(
        functools.partial(_attn_kernel, l=l),
        out_shape=jax.ShapeDtypeStruct((NT, ATTN_WIDTH), jnp.bfloat16),
        grid=(NT // ATT_QT, N_KV_HEADS),
        in_specs=[pl.BlockSpec(memory_space=pltpu.SMEM),
                  pl.BlockSpec((ATT_QT, Q_GROUP * HEAD_DIM), lambda i, h: (i, h)),
                  pl.BlockSpec((blk, HEAD_DIM), prev_map(kcol)),
                  pl.BlockSpec((ATT_QT, HEAD_DIM), cur_map(kcol)),
                  pl.BlockSpec((blk, HEAD_DIM), next_map(kcol)),
                  pl.BlockSpec((blk, HEAD_DIM), prev_map(vcol)),
                  pl.BlockSpec((ATT_QT, HEAD_DIM), cur_map(vcol)),
                  pl.BlockSpec((blk, HEAD_DIM), next_map(vcol)),
                  pl.BlockSpec((CTX_LEN, HEAD_DIM), ctx_map(kcol)),
                  pl.BlockSpec((CTX_LEN, HEAD_DIM), ctx_map(vcol))],
        out_specs=pl.BlockSpec((ATT_QT, Q_GROUP * HEAD_DIM), lambda i, h: (i, h)),
        compiler_params=_cparams(2),
        name="window_attention",
    )(sink, qkv, qkv, qkv, qkv, qkv, qkv, qkv, qkv, qkv)


def _mix_kernel(a_ref, yh_ref, yf_ref, g_ref, x_ref, wa_ref, wh_ref, wf_ref, wo_ref,
                g1_ref, sh2_ref, sc2_ref, lg_ref, lb_ref, xo_ref, t_ref, y_ref):
    tn = 512
    for c in range(D_MODEL // tn):
        sl = slice(c * tn, (c + 1) * tn)
        ya = jnp.dot(a_ref[...], wa_ref[:, sl], preferred_element_type=jnp.float32)
        yh = jnp.dot(yh_ref[...], wh_ref[:, sl], preferred_element_type=jnp.float32)
        yf = jnp.dot(yf_ref[...], wf_ref[:, sl], preferred_element_type=jnp.float32)
        y = (g_ref[:, sl].astype(jnp.float32) * ya
             + g_ref[:, D_MODEL + c * tn:D_MODEL + (c + 1) * tn].astype(jnp.float32) * yh
             + g_ref[:, 2 * D_MODEL + c * tn:2 * D_MODEL + (c + 1) * tn].astype(jnp.float32) * yf)
        y_ref[:, sl] = y.astype(y_ref.dtype)
    o = jnp.dot(y_ref[...], wo_ref[...], preferred_element_type=jnp.float32)
    xn = _ln(DEEPNORM_ALPHA * x_ref[...] + g1_ref[...] * o) * lg_ref[...] + lb_ref[...]
    xo_ref[...] = xn
    t_ref[...] = (_ln(xn) * (1.0 + sc2_ref[...]) + sh2_ref[...]).astype(t_ref.dtype)


def mix_call(a, yh, yf, gates, x, wa, wh, wf, wo, mod4, ln_g, ln_b, l):
    tm = 256
    row = lambda i: (i, 0)
    full = lambda i: (0, 0)
    const = pl.Buffered(1)
    return pl.pallas_call(
        _mix_kernel,
        out_shape=(jax.ShapeDtypeStruct((NT, D_MODEL), jnp.float32),
                   jax.ShapeDtypeStruct((NT, D_MODEL), jnp.bfloat16)),
        grid=(NT // tm,),
        in_specs=[pl.BlockSpec((tm, ATTN_WIDTH), row),
                  pl.BlockSpec((tm, HYENA_WIDTH), row),
                  pl.BlockSpec((tm, FNET_WIDTH), row),
                  pl.BlockSpec((tm, 3 * D_MODEL), row),
                  pl.BlockSpec((tm, D_MODEL), row),
                  pl.BlockSpec((ATTN_WIDTH, D_MODEL), full, pipeline_mode=const),
                  pl.BlockSpec((HYENA_WIDTH, D_MODEL), full, pipeline_mode=const),
                  pl.BlockSpec((FNET_WIDTH, D_MODEL), full, pipeline_mode=const),
                  pl.BlockSpec((D_MODEL, D_MODEL), full, pipeline_mode=const),
                  _mod_spec(l, 2, tm), _mod_spec(l, 3, tm), _mod_spec(l, 4, tm),
                  pl.BlockSpec((None, 1, D_MODEL), lambda i: (l, 0, 0)),
                  pl.BlockSpec((None, 1, D_MODEL), lambda i: (l, 0, 0))],
        out_specs=(pl.BlockSpec((tm, D_MODEL), row), pl.BlockSpec((tm, D_MODEL), row)),
        scratch_shapes=[pltpu.VMEM((tm, D_MODEL), jnp.bfloat16)],
        compiler_params=_cparams(1),
        name="mixer_out",
    )(a, yh, yf, gates, x, wa, wh, wf, wo, mod4, mod4, mod4, ln_g, ln_b)


ROUTE_LANES = LANE


def _route_kernel(t_ref, w_ref, b_ref, o_ref):
    logits = jnp.dot(t_ref[...], w_ref[...], preferred_element_type=jnp.float32) + b_ref[...]
    lane = lax.broadcasted_iota(jnp.int32, logits.shape, 1)
    neg = -jnp.inf
    big = jnp.int32(2 ** 30)

    def first_argmax(vals):
        mx = jnp.max(vals, -1, keepdims=True)
        idx = jnp.min(jnp.where(vals == mx, lane, big), -1, keepdims=True)
        return mx, idx

    g_vals = jnp.where(lane < N_GROUPS, logits, neg)
    g_max, grp = first_argmax(g_vals)
    g_w = 1.0 / jnp.sum(jnp.exp(g_vals - g_max), -1, keepdims=True)
    lo = N_GROUPS + grp * EXPERTS_PER_GROUP
    e_vals = jnp.where((lane >= lo) & (lane < lo + EXPERTS_PER_GROUP), logits, neg)
    v1, i1 = first_argmax(e_vals)
    v2, i2 = first_argmax(jnp.where(lane == i1, neg, e_vals))
    e2 = jnp.exp(v2 - v1)
    w1 = g_w / (1.0 + e2)
    w2 = g_w * e2 / (1.0 + e2)
    id1 = (i1 - N_GROUPS).astype(jnp.float32)
    id2 = (i2 - N_GROUPS).astype(jnp.float32)
    o_ref[...] = jnp.where(lane == 0, id1, jnp.where(lane == 1, id2,
                           jnp.where(lane == 2, w1, jnp.where(lane == 3, w2, 0.0))))


def route_call(t, wr, br):
    tm = 512
    return pl.pallas_call(
        _route_kernel,
        out_shape=jax.ShapeDtypeStruct((NT, ROUTE_LANES), jnp.float32),
        grid=(NT // tm,),
        in_specs=[pl.BlockSpec((tm, D_MODEL), lambda i: (i, 0)),
                  pl.BlockSpec((D_MODEL, ROUTE_LANES), lambda i: (0, 0)),
                  pl.BlockSpec((1, ROUTE_LANES), lambda i: (0, 0))],
        out_specs=pl.BlockSpec((tm, ROUTE_LANES), lambda i: (i, 0)),
        compiler_params=_cparams(1),
        name="moe_router",
    )(t, wr, br)


def _moe_kernel(be_ref, nu_ref, x_ref, w1_ref, w3_ref, w2_ref, o_ref, b1_ref, b3_ref, b2_ref):
    i = pl.program_id(0)
    first = (i == 0) | (be_ref[i] != be_ref[jnp.maximum(i - 1, 0)])

    @pl.when(first)
    def _():
        b1_ref[...] = w1_ref[...].astype(jnp.bfloat16)
        b3_ref[...] = w3_ref[...].astype(jnp.bfloat16)
        b2_ref[...] = w2_ref[...].astype(jnp.bfloat16)

    @pl.when(i < nu_ref[0])
    def _():
        x = x_ref[...]
        h1 = jnp.dot(x, b1_ref[...], preferred_element_type=jnp.float32)
        h3 = jnp.dot(x, b3_ref[...], preferred_element_type=jnp.float32)
        hh = (h1 * jax.nn.sigmoid(h1) * h3).astype(jnp.bfloat16)
        o_ref[...] = jnp.dot(hh, b2_ref[...], preferred_element_type=jnp.float32)

    @pl.when(i >= nu_ref[0])
    def _():
        o_ref[...] = jnp.zeros_like(o_ref)


def moe_call(block_e, n_used, xb, w1, w3, w2, l):
    grid_spec = pltpu.PrefetchScalarGridSpec(
        num_scalar_prefetch=2,
        grid=(MOE_BLOCKS,),
        in_specs=[pl.BlockSpec((MOE_TB, D_MODEL), lambda i, be, nu: (i, 0)),
                  pl.BlockSpec((None, None, D_MODEL, EXPERT_HIDDEN), lambda i, be, nu: (l, be[i], 0, 0)),
                  pl.BlockSpec((None, None, D_MODEL, EXPERT_HIDDEN), lambda i, be, nu: (l, be[i], 0, 0)),
                  pl.BlockSpec((None, None, EXPERT_HIDDEN, D_MODEL), lambda i, be, nu: (l, be[i], 0, 0))],
        out_specs=pl.BlockSpec((MOE_TB, D_MODEL), lambda i, be, nu: (i, 0)),
        scratch_shapes=[pltpu.VMEM((D_MODEL, EXPERT_HIDDEN), jnp.bfloat16),
                        pltpu.VMEM((D_MODEL, EXPERT_HIDDEN), jnp.bfloat16),
                        pltpu.VMEM((EXPERT_HIDDEN, D_MODEL), jnp.bfloat16)])
    return pl.pallas_call(
        _moe_kernel,
        out_shape=jax.ShapeDtypeStruct((MOE_SLOTS, D_MODEL), jnp.float32),
        grid_spec=grid_spec,
        compiler_params=_cparams(1),
        name="moe_experts",
    )(block_e, n_used, xb, w1, w3, w2)


def _post_kernel(x_ref, y0_ref, y1_ref, r_ref, g2_ref, lg_ref, lb_ref, sh_ref, sc_ref,
                 xo_ref, h_ref):
    r = r_ref[...]
    m = r[:, 2:3] * y0_ref[...] + r[:, 3:4] * y1_ref[...]
    xn = _ln(DEEPNORM_ALPHA * x_ref[...] + g2_ref[...] * m) * lg_ref[...] + lb_ref[...]
    xo_ref[...] = xn
    h_ref[...] = (_ln(xn) * (1.0 + sc_ref[...]) + sh_ref[...]).astype(h_ref.dtype)


def post_call(x, y0, y1, route, mod4, ln_g, ln_b, l):
    tm = 512
    ln = min(l + 1, DEPTH - 1)
    row = lambda i: (i, 0)
    return pl.pallas_call(
        _post_kernel,
        out_shape=(jax.ShapeDtypeStruct((NT, D_MODEL), jnp.float32),
                   jax.ShapeDtypeStruct((NT, D_MODEL), jnp.bfloat16)),
        grid=(NT // tm,),
        in_specs=[pl.BlockSpec((tm, D_MODEL), row),
                  pl.BlockSpec((tm, D_MODEL), row),
                  pl.BlockSpec((tm, D_MODEL), row),
                  pl.BlockSpec((tm, ROUTE_LANES), row),
                  _mod_spec(l, 5, tm),
                  pl.BlockSpec((None, 1, D_MODEL), lambda i: (l, 0, 0)),
                  pl.BlockSpec((None, 1, D_MODEL), lambda i: (l, 0, 0)),
                  _mod_spec(ln, 0, tm), _mod_spec(ln, 1, tm)],
        out_specs=(pl.BlockSpec((tm, D_MODEL), row), pl.BlockSpec((tm, D_MODEL), row)),
        compiler_params=_cparams(1),
        name="moe_combine_postln",
    )(x, y0, y1, route, mod4, ln_g, ln_b, mod4, mod4)


CH = HYENA_WIDTH


def _hi_lo(x):
    hi = x.astype(jnp.bfloat16)
    lo = (x - hi.astype(jnp.float32)).astype(jnp.bfloat16)
    return hi, lo


def _const_lhs3(m):
    hi, lo = _hi_lo(jnp.asarray(m, jnp.float32))
    return jnp.concatenate([hi, lo, hi], 1)


def _data_rhs3(x):
    hi, lo = _hi_lo(x)
    return jnp.concatenate([hi, hi, lo], 0)


def _const_rhs3(m):
    hi, lo = _hi_lo(jnp.asarray(m, jnp.float32))
    return jnp.concatenate([hi, hi, lo], 0)


def _data_lhs3(x):
    hi, lo = _hi_lo(x)
    return jnp.concatenate([hi, lo, hi], 1)


def _cs(n, rows, cols):
    ang = 2.0 * np.pi * np.outer(np.arange(rows), np.arange(cols)) / n
    return np.cos(ang), np.sin(ang)


def dft_consts(n_fft, n1):
    n2 = n_fft // n1
    c1, s1 = _cs(n1, n1, n1)
    c2, s2 = _cs(n2, n2, n2)
    h = n1 // 2
    tr, ti = _cs(n_fft, n1, n2)
    return dict(
        n1=n1, n2=n2,
        lead_real=_const_lhs3(np.concatenate([c1, -s1], 0)),
        lead_pair=_const_lhs3(np.block([[c1[:, :h], s1[:, :h]], [-s1[:, :h], c1[:, :h]]])),
        lead_inv=_const_lhs3(np.block([[c1[:h, :], -s1[:h, :]], [s1[:h, :], c1[:h, :]]]) / n_fft),
        f2=_const_lhs3(np.block([[c2, s2], [-s2, c2]])),
        f2_inv=_const_lhs3(np.block([[c2, -s2], [s2, c2]])),
        tw_r=jnp.asarray(tr[:, :, None], jnp.float32),
        tw_i=jnp.asarray(-ti[:, :, None], jnp.float32),
    )


def _lead_kernel(m_ref, x_ref, o_ref):
    o_ref[...] = jnp.dot(m_ref[...], _data_rhs3(x_ref[...]), preferred_element_type=jnp.float32)


def lead_call(mcat, x2d, k, row_off, groups, name):
    mo = mcat.shape[0]
    width = x2d.shape[1]
    tn = min(width, 8192)
    off = row_off // k
    return pl.pallas_call(
        _lead_kernel,
        out_shape=jax.ShapeDtypeStruct((groups, mo, width), jnp.float32),
        grid=(groups, width // tn),
        in_specs=[pl.BlockSpec(mcat.shape, lambda g, j: (0, 0)),
                  pl.BlockSpec((k, tn), lambda g, j: (off + g, j))],
        out_specs=pl.BlockSpec((None, mo, tn), lambda g, j: (g, 0, j)),
        compiler_params=_cparams(2),
        name=name,
    )(mcat, x2d)


def _lead_out_kernel(m_ref, d_ref, z_ref, x0_ref, db_ref, s_ref, o_ref):
    acc = jnp.dot(m_ref[...], _data_rhs3(d_ref[...]), preferred_element_type=jnp.float32)
    y = acc * (1.0 / (s_ref[...] + 1e-6)) + z_ref[...] * db_ref[...]
    o_ref[...] = (x0_ref[...] * y).astype(o_ref.dtype)


def lead_out_call(mcat, d3, z2d, x02d, dbias_t, ssum_t, k_out, row_off, name):
    groups, k_in, width = d3.shape
    tn = min(width, 8192)
    off = row_off // k_out
    view = pl.BlockSpec((k_out, tn), lambda g, j: (off + g, j))
    lanes = pl.BlockSpec((1, tn), lambda g, j: (0, 0))
    return pl.pallas_call(
        _lead_out_kernel,
        out_shape=jax.ShapeDtypeStruct((groups, k_out, width), jnp.bfloat16),
        grid=(groups, width // tn),
        in_specs=[pl.BlockSpec(mcat.shape, lambda g, j: (0, 0)),
                  pl.BlockSpec((None, k_in, tn), lambda g, j: (g, 0, j)),
                  view, view, lanes, lanes],
        out_specs=pl.BlockSpec((None, k_out, tn), lambda g, j: (g, 0, j)),
        compiler_params=_cparams(2),
        name=name,
    )(mcat, d3, z2d, x02d, dbias_t, ssum_t)


def _hy_slab_kernel(*refs, kb, n2, conv):
    if conv:
        ar_ref, ai_ref, tr_ref, ti_ref, f_ref, fi_ref, hr_ref, hi_ref, o_ref = refs
    else:
        ar_ref, ai_ref, tr_ref, ti_ref, f_ref, o_ref = refs
    f = f_ref[...]
    for j in range(kb):
        ar, ai, tr, ti = ar_ref[j], ai_ref[j], tr_ref[j], ti_ref[j]
        br = ar * tr - ai * ti
        bi = ar * ti + ai * tr
        x = jnp.dot(f, _data_rhs3(jnp.concatenate([br, bi], 0)), preferred_element_type=jnp.float32)
        xr, xi = x[:n2], x[n2:]
        if conv:
            hr, hi = hr_ref[j], hi_ref[j]
            yr = xr * hr - xi * hi
            yi = xr * hi + xi * hr
            d = jnp.dot(fi_ref[...], _data_rhs3(jnp.concatenate([yr, yi], 0)),
                        preferred_element_type=jnp.float32)
            dr, di = d[:n2], d[n2:]
            o_ref[0, j] = dr * tr + di * ti
            o_ref[1, j] = di * tr - dr * ti
        else:
            o_ref[0, j] = xr
            o_ref[1, j] = xi


def hy_slab_call(a3, cst, h5, name):
    n1, n2 = cst["n1"], cst["n2"]
    groups = a3.shape[0]
    kb = 8
    a5 = a3.reshape(groups, 2, n1, n2, CH)
    slab = lambda part: pl.BlockSpec((None, None, kb, n2, CH), lambda b, g: (g, part, b, 0, 0))
    tw = pl.BlockSpec((kb, n2, 1), lambda b, g: (b, 0, 0))
    mat = pl.BlockSpec(cst["f2"].shape, lambda b, g: (0, 0))
    in_specs = [slab(0), slab(1), tw, tw, mat]
    args = [a5, a5, cst["tw_r"], cst["tw_i"], cst["f2"]]
    conv = h5 is not None
    if conv:
        hs = lambda part: pl.BlockSpec((None, None, kb, n2, CH), lambda b, g: (0, part, b, 0, 0))
        in_specs += [mat, hs(0), hs(1)]
        args += [cst["f2_inv"], h5, h5]
    return pl.pallas_call(
        functools.partial(_hy_slab_kernel, kb=kb, n2=n2, conv=conv),
        out_shape=jax.ShapeDtypeStruct((groups, 2, n1, n2, CH), jnp.float32),
        grid=(n1 // kb, groups),
        in_specs=in_specs,
        out_specs=pl.BlockSpec((None, 2, kb, n2, CH), lambda b, g: (g, 0, b, 0, 0)),
        compiler_params=_cparams(2),
        name=name,
    )(*args)


def filter_tables(n):
    m = np.arange(2 * n)
    pos = np.where(m < n, m, 2 * n - m)
    pos = np.where(m == n, 0, pos)
    t = pos / (n - 1.0)
    w = 2.0 * np.pi * pos / n
    bands = np.linspace(1e-4, HYENA_BANDS - 1, HYENA_BANDS)
    feat = np.zeros((2 * n, LANE))
    feat[:, 0] = t
    feat[:, 1:1 + HYENA_BANDS] = np.cos(bands[None, :] * w[:, None])
    feat[:, 1 + HYENA_BANDS:HYENA_EMB] = -np.sin(bands[None, :] * w[:, None])
    aux = np.zeros((2 * n, LANE))
    aux[:, 0] = t
    aux[:, 1] = (m < n)
    aux[:, 2] = (m > n)
    deltas = np.abs(np.linspace(HYENA_MIN_DECAY, HYENA_MAX_DECAY, HYENA_WIDTH))[None, :]
    return (jnp.asarray(feat, jnp.float32), jnp.asarray(aux, jnp.float32),
            jnp.asarray(deltas, jnp.float32))


def _filt_kernel(z_ref, aux_ref, dl_ref, w1_ref, b1_ref, w2_ref, b2_ref, w3_ref, b3_ref,
                 w4_ref, b4_ref, fr_ref, o_ref, s_ref):
    hp = lax.Precision.HIGHEST
    f32 = jnp.float32
    fr = fr_ref[...]
    h = jnp.sin(fr[0:1] * (jnp.dot(z_ref[...], w1_ref[...], precision=hp, preferred_element_type=f32)
                           + b1_ref[...]))
    h = jnp.sin(fr[1:2] * (jnp.dot(h, w2_ref[...], precision=hp, preferred_element_type=f32)
                           + b2_ref[...]))
    h = jnp.sin(fr[2:3] * (jnp.dot(h, w3_ref[...], precision=hp, preferred_element_type=f32)
                           + b3_ref[...]))
    f = jnp.dot(h, w4_ref[...], precision=hp, preferred_element_type=f32) + b4_ref[...]
    aux = aux_ref[...]
    dec = jnp.exp(-aux[:, 0:1] * dl_ref[...])
    c = (f[:, :HYENA_WIDTH] * aux[:, 1:2] + f[:, HYENA_WIDTH:] * aux[:, 2:3]) * dec
    o_ref[...] = c

    @pl.when(pl.program_id(0) == 0)
    def _():
        s_ref[...] = jnp.zeros_like(s_ref)

    s_ref[...] += jnp.sum(jnp.abs(c), 0, keepdims=True)


def filter_call(n, tabs, w1p, b1, w2, b2, w3, b3, w4, b4, freq):
    feat, aux, deltas = tabs
    tr = min(2 * n, 1024)
    row = lambda i: (i, 0)
    full = lambda a: pl.BlockSpec(a.shape, lambda i: (0,) * a.ndim)
    ws = [w1p, b1, w2, b2, w3, b3, w4, b4, freq]
    return pl.pallas_call(
        _filt_kernel,
        out_shape=(jax.ShapeDtypeStruct((2 * n, HYENA_WIDTH), jnp.float32),
                   jax.ShapeDtypeStruct((1, HYENA_WIDTH), jnp.float32)),
        grid=(2 * n // tr,),
        in_specs=[pl.BlockSpec((tr, LANE), row), pl.BlockSpec((tr, LANE), row), full(deltas)]
                 + [full(a) for a in ws],
        out_specs=(pl.BlockSpec((tr, HYENA_WIDTH), row),
                   pl.BlockSpec((1, HYENA_WIDTH), lambda i: (0, 0))),
        compiler_params=_cparams(1),
        name="hyena_filter",
    )(feat, aux, deltas, *ws)


def fnet_consts(n, n1):
    cst = dft_consts(n, n1)
    c, s = _cs(FNET_GROUP_DIM, FNET_GROUP_DIM, FNET_GROUP_DIM)
    cst["chan"] = _const_rhs3(np.concatenate([c, s], 0) / math.sqrt(n * FNET_GROUP_DIM))
    return cst


def _fn_slab_kernel(ar_ref, ai_ref, tr_ref, ti_ref, f_ref, cs_ref, o_ref, *, kb, n2):
    f = f_ref[...]
    cs = cs_ref[...]
    gd = FNET_GROUP_DIM
    for j in range(kb):
        ar, ai, tr, ti = ar_ref[j], ai_ref[j], tr_ref[j], ti_ref[j]
        br = ar * tr - ai * ti
        bi = ar * ti + ai * tr
        v = jnp.dot(f, _data_rhs3(jnp.concatenate([br, bi], 0)), preferred_element_type=jnp.float32)
        vr, vi = v[:n2], v[n2:]
        for g in range(FNET_GROUPS):
            lhs = jnp.concatenate([vr[:, g * gd:(g + 1) * gd], vi[:, g * gd:(g + 1) * gd]], 1)
            y = jnp.dot(_data_lhs3(lhs), cs, preferred_element_type=jnp.float32)
            o_ref[:, j * CH + g * gd:j * CH + (g + 1) * gd] = y.astype(o_ref.dtype)


def fn_slab_call(a3, cst, name):
    n1, n2 = cst["n1"], cst["n2"]
    groups = a3.shape[0]
    kb = 8
    a5 = a3.reshape(groups, 2, n1, n2, CH)
    slab = lambda part: pl.BlockSpec((None, None, kb, n2, CH), lambda b, g: (g, part, b, 0, 0))
    tw = pl.BlockSpec((kb, n2, 1), lambda b, g: (b, 0, 0))
    return pl.pallas_call(
        functools.partial(_fn_slab_kernel, kb=kb, n2=n2),
        out_shape=jax.ShapeDtypeStruct((groups, n2, n1 * CH), jnp.bfloat16),
        grid=(n1 // kb, groups),
        in_specs=[slab(0), slab(1), tw, tw,
                  pl.BlockSpec(cst["f2"].shape, lambda b, g: (0, 0)),
                  pl.BlockSpec(cst["chan"].shape, lambda b, g: (0, 0))],
        out_specs=pl.BlockSpec((None, n2, kb * CH), lambda b, g: (g, 0, b)),
        compiler_params=_cparams(2),
        name=name,
    )(a5, a5, cst["tw_r"], cst["tw_i"], cst["f2"], cst["chan"])


def hyena_long(z, x0, n, batch, row_off, cst, tabs, filt_params, d_bias, tag):
    n1, n2 = cst["n1"], cst["n2"]
    w1, b1, w2, b2, w3, b3, w4, b4, freq = filt_params
    w1p = jnp.concatenate([w1, jnp.zeros((LANE - HYENA_EMB, HYENA_FILTER_HIDDEN), w1.dtype)], 0)
    circ, ssum = filter_call(n, tabs, w1p, b1[None, :], w2, b2[None, :], w3, b3[None, :],
                             w4, b4[None, :], freq)
    width = n2 * CH
    h3 = lead_call(cst["lead_real"], circ.reshape(n1, width), n1, 0, 1, "hyena_filt_dft1_" + tag)
    h5 = hy_slab_call(h3, cst, None, "hyena_filt_dft2_" + tag)
    z2d = z.reshape(-1, width)
    x02d = x0.reshape(-1, width)
    pairs = batch // 2
    a3 = lead_call(cst["lead_pair"], z2d, n1, row_off // n2, pairs, "hyena_dft1_" + tag)
    d5 = hy_slab_call(a3, cst, h5, "hyena_conv_" + tag)
    reps = width // CH if width <= 8192 else 8192 // CH
    y = lead_out_call(cst["lead_inv"], d5.reshape(pairs, 2 * n1, width), z2d, x02d,
                      jnp.tile(d_bias[None, :], (1, reps)), jnp.tile(ssum, (1, reps)),
                      n1, row_off // n2, "hyena_idft1_" + tag)
    return y.reshape(batch * n, CH)


def fnet_seq(u, n, batch, row_off, cst, tag):
    n1, n2 = cst["n1"], cst["n2"]
    width = n2 * CH
    a3 = lead_call(cst["lead_real"], u.reshape(-1, width), n1, row_off // n2, batch, "fnet_dft1_" + tag)
    return fn_slab_call(a3, cst, "fnet_dft2_" + tag).reshape(batch * n, CH)


SC_TM = CTX_LEN


def _sconv_kernel(u_ref, p_ref, n_ref, w_ref, b_ref, z_ref, x0_ref, *, lat_tiles, tiles_per_seq):
    i = pl.program_id(0)
    pos = i % tiles_per_seq
    is_lat = i < lat_tiles
    keep_prev = jnp.where(is_lat & (pos != 0), 1.0, 0.0)
    keep_next = jnp.where(is_lat & (pos != tiles_per_seq - 1), 1.0, 0.0)
    u = u_ref[...]
    tm = u.shape[0]
    row = lax.broadcasted_iota(jnp.int32, u.shape, 0)
    um = jnp.where(row == 0, p_ref[7:8, :] * keep_prev, pltpu.roll(u, 1, 0))
    up = jnp.where(row == tm - 1, n_ref[0:1, :] * keep_next, pltpu.roll(u, tm - 1, 0))
    w = w_ref[...]
    uc = b_ref[...] + um * w[0:1] + u * w[1:2] + up * w[2:3]
    x0_ref[...] = uc[:, :CH]
    z_ref[...] = uc[:, CH:2 * CH] * uc[:, 2 * CH:]


def sconv_call(u, conv_w, conv_b, l, n_lat, seq):
    rows = u.shape[0]
    tm = SC_TM
    per8 = tm // 8
    last8 = rows // 8 - 1
    out = jax.ShapeDtypeStruct((rows, CH), jnp.float32)
    return pl.pallas_call(
        functools.partial(_sconv_kernel, lat_tiles=n_lat // tm, tiles_per_seq=seq // tm),
        out_shape=(out, out),
        grid=(rows // tm,),
        in_specs=[pl.BlockSpec((tm, 3 * CH), lambda i: (i, 0)),
                  pl.BlockSpec((8, 3 * CH), lambda i: (jnp.maximum(i * per8 - 1, 0), 0)),
                  pl.BlockSpec((8, 3 * CH), lambda i: (jnp.minimum((i + 1) * per8, last8), 0)),
                  pl.BlockSpec((None, 3, 3 * CH), lambda i: (l, 0, 0)),
                  pl.BlockSpec((None, 1, 3 * CH), lambda i: (l, 0, 0))],
        out_specs=(pl.BlockSpec((tm, CH), lambda i: (i, 0)), pl.BlockSpec((tm, CH), lambda i: (i, 0))),
        compiler_params=_cparams(1),
        name="hyena_short_conv",
    )(u, u, u, conv_w, conv_b)


def _dispatch(route):
    e_flat = route[:, :TOP_K].astype(jnp.int32).reshape(-1)
    onehot = (e_flat[:, None] == jnp.arange(N_EXPERTS, dtype=jnp.int32)[None, :]).astype(jnp.int32)
    csum = jnp.cumsum(onehot, 0)
    rank = jnp.sum((csum - onehot) * onehot, 1)
    counts = csum[-1]
    padded = ((counts + MOE_TB - 1) // MOE_TB) * MOE_TB
    pend = jnp.cumsum(padded)
    pstart = pend - padded
    dest = pstart[e_flat] + rank
    tok = jnp.arange(NT * TOP_K, dtype=jnp.int32) // TOP_K
    slot_tok = jnp.zeros((MOE_SLOTS,), jnp.int32).at[dest].set(tok)
    block_e = jnp.minimum(jnp.searchsorted(pend, jnp.arange(MOE_BLOCKS, dtype=jnp.int32) * MOE_TB,
                                           side='right'), N_EXPERTS - 1).astype(jnp.int32)
    n_used = (pend[-1] // MOE_TB).astype(jnp.int32).reshape(1)
    return slot_tok, dest.reshape(NT, TOP_K), block_e, n_used


def kernel(x, c, ctx, c_ctx, w_ada, b_ada, w_in, conv_w, conv_b, hy_w1, hy_b1, hy_w2, hy_b2, hy_w3, hy_b3, hy_w4, hy_b4, hy_freq, hy_dbias, attn_sink, w_br_attn, w_br_hyena, w_br_fnet, w_out, ln1_g, ln1_b, ln2_g, ln2_b, rg_w, rg_b, re_w, re_b, moe_w1, moe_w3, moe_w2):
    bf16 = jnp.bfloat16
    xs = jnp.concatenate([x.reshape(N_LAT, D_MODEL), ctx.reshape(N_CTX, D_MODEL)], 0)
    c8 = jnp.concatenate([c, c_ctx[None, :], jnp.zeros((MOD_ROWS - BATCH - 1, D_MODEL), c.dtype)], 0)
    mod4 = ada_call(c8, w_ada, b_ada).reshape(DEPTH, MOD_ROWS, 1, 6 * D_MODEL)
    cos_t, sin_t = rope_tables(512)
    ln1_g3, ln1_b3 = ln1_g.reshape(DEPTH, 1, D_MODEL), ln1_b.reshape(DEPTH, 1, D_MODEL)
    ln2_g3, ln2_b3 = ln2_g.reshape(DEPTH, 1, D_MODEL), ln2_b.reshape(DEPTH, 1, D_MODEL)
    pad = jnp.zeros((DEPTH, D_MODEL, ROUTE_LANES - N_GROUPS - N_EXPERTS), jnp.float32)
    w_route = jnp.concatenate([rg_w, re_w, pad], -1).astype(bf16)
    b_route = jnp.concatenate([rg_b, re_b, pad[:, 0]], -1).reshape(DEPTH, 1, ROUTE_LANES)

    hy_lat, hy_ctx = dft_consts(2 * SEQ, 64), dft_consts(2 * CTX_LEN, 16)
    fn_lat, fn_ctx = fnet_consts(SEQ, 64), fnet_consts(CTX_LEN, 16)
    tab_lat, tab_ctx = filter_tables(SEQ), filter_tables(CTX_LEN)
    conv_b3 = conv_b.reshape(DEPTH, 1, 3 * HYENA_WIDTH)

    h = pre_call(xs, mod4)
    for l in range(DEPTH):
        qkv = qkv_call(h, w_in, l, cos_t, sin_t)
        u_hy = proj_call(h, w_in, l, HY_OFF, 3 * HYENA_WIDTH, None, jnp.float32, "proj_hyena")
        u_fn = proj_call(h, w_in, l, FN_OFF, FNET_WIDTH, None, jnp.float32, "proj_fnet")
        gates = proj_call(h, w_in, l, GATE_OFF, 3 * D_MODEL, "sigmoid", bf16, "proj_gates")
        a = attn_call(qkv, attn_sink, l)
        filt = (hy_w1[l], hy_b1[l], hy_w2[l], hy_b2[l], hy_w3[l], hy_b3[l], hy_w4[l], hy_b4[l], hy_freq[l])
        z, x0 = sconv_call(u_hy, conv_w, conv_b3, l, N_LAT, SEQ)
        y_hy = jnp.concatenate(
            [hyena_long(z, x0, SEQ, BATCH, 0, hy_lat, tab_lat, filt, hy_dbias[l], "lat"),
             hyena_long(z, x0, CTX_LEN, BATCH, N_LAT, hy_ctx, tab_ctx, filt, hy_dbias[l], "ctx")], 0)
        y_fn = jnp.concatenate([fnet_seq(u_fn, SEQ, BATCH, 0, fn_lat, "lat"),
                                fnet_seq(u_fn, CTX_LEN, BATCH, N_LAT, fn_ctx, "ctx")], 0)
        xs, t = mix_call(a, y_hy, y_fn, gates, xs,
                         w_br_attn[l].astype(bf16), w_br_hyena[l].astype(bf16),
                         w_br_fnet[l].astype(bf16), w_out[l].astype(bf16),
                         mod4, ln1_g3, ln1_b3, l)
        route = route_call(t, w_route[l], b_route[l])
        slot_tok, dest, block_e, n_used = _dispatch(route)
        yb = moe_call(block_e, n_used, t[slot_tok], moe_w1, moe_w3, moe_w2, l)
        xs, h = post_call(xs, yb[dest[:, 0]], yb[dest[:, 1]], route, mod4, ln2_g3, ln2_b3, l)
    return xs[:N_LAT].reshape(BATCH, SEQ, D_MODEL)
```

```python
import functools
import math

import numpy as np
import jax
import jax.numpy as jnp
from jax import lax
from jax.experimental import pallas as pl
from jax.experimental.pallas import tpu as pltpu

D_MODEL = 2048
BATCH = 4
SEQ = 4096
DEPTH = 4
GRID_W = 64
CTX_LEN = 256

N_HEADS = 8
N_KV_HEADS = 2
HEAD_DIM = 128
Q_GROUP = N_HEADS // N_KV_HEADS
ATTN_WIDTH = N_HEADS * HEAD_DIM
KV_WIDTH = N_KV_HEADS * HEAD_DIM
WINDOW = 128
ROPE_THETA = 10000.0
MASK_VALUE = -1e30

HYENA_WIDTH = 512
HYENA_SHORT = 3
HYENA_BANDS = 16
HYENA_EMB = 1 + 2 * HYENA_BANDS
HYENA_FILTER_HIDDEN = 64
HYENA_TARGET = 1e-2
HYENA_MAX_DECAY = math.log(HYENA_TARGET) / 0.3
HYENA_MIN_DECAY = math.log(HYENA_TARGET) / 1.5

FNET_WIDTH = 512
FNET_GROUPS = 4
FNET_GROUP_DIM = FNET_WIDTH // FNET_GROUPS

Q_OFF = 0
K_OFF = Q_OFF + ATTN_WIDTH
V_OFF = K_OFF + KV_WIDTH
HY_OFF = V_OFF + KV_WIDTH
FN_OFF = HY_OFF + 3 * HYENA_WIDTH
GATE_OFF = FN_OFF + FNET_WIDTH
IN_WIDTH = GATE_OFF + 3 * D_MODEL
QKV_WIDTH = HY_OFF

N_GROUPS = 4
EXPERTS_PER_GROUP = 8
N_EXPERTS = N_GROUPS * EXPERTS_PER_GROUP
TOP_K = 2
EXPERT_HIDDEN = 512

LN_EPS = 1e-6
DEEPNORM_ALPHA = (2 * DEPTH) ** 0.25

N_LAT = BATCH * SEQ
N_CTX = BATCH * CTX_LEN
NT = N_LAT + N_CTX
MOD_ROWS = 8

LANE = 128
VMEM_LIMIT = 56 * 1024 * 1024

MOE_TB = 256
MOE_BLOCKS = -(-(NT * TOP_K + N_EXPERTS * (MOE_TB - 1)) // MOE_TB)
MOE_SLOTS = MOE_BLOCKS * MOE_TB


def _cparams(n_axes):
    return pltpu.CompilerParams(dimension_semantics=("arbitrary",) * n_axes,
                                vmem_limit_bytes=VMEM_LIMIT)


def _mod_row(i, tm):
    return jnp.minimum((i * tm) // SEQ, BATCH)


def _mod_spec(l, chunk, tm, axis=0):
    def imap(*g):
        return (l, _mod_row(g[axis], tm), 0, chunk)
    return pl.BlockSpec((None, None, 1, D_MODEL), imap)


def _ln(x):
    mu = jnp.mean(x, -1, keepdims=True)
    xc = x - mu
    var = jnp.mean(xc * xc, -1, keepdims=True)
    return xc * lax.rsqrt(var + LN_EPS)


def _ada_kernel(c_ref, w_ref, b_ref, o_ref):
    c = c_ref[...]
    a = (c * jax.nn.sigmoid(c)).astype(jnp.bfloat16)
    o_ref[...] = jnp.dot(a, w_ref[...].astype(jnp.bfloat16),
                         preferred_element_type=jnp.float32) + b_ref[...]


def ada_call(c8, w_ada, b_ada):
    tn = 1024
    n = 6 * D_MODEL
    return pl.pallas_call(
        _ada_kernel,
        out_shape=jax.ShapeDtypeStruct((DEPTH, MOD_ROWS, n), jnp.float32),
        grid=(DEPTH, n // tn),
        in_specs=[pl.BlockSpec((MOD_ROWS, D_MODEL), lambda l, j: (0, 0)),
                  pl.BlockSpec((None, D_MODEL, tn), lambda l, j: (l, 0, j)),
                  pl.BlockSpec((None, 1, tn), lambda l, j: (l, 0, j))],
        out_specs=pl.BlockSpec((None, MOD_ROWS, tn), lambda l, j: (l, 0, j)),
        compiler_params=_cparams(2),
        name="adaln",
    )(c8, w_ada, b_ada.reshape(DEPTH, 1, n))


def _pre_kernel(x_ref, sh_ref, sc_ref, h_ref):
    h_ref[...] = (_ln(x_ref[...]) * (1.0 + sc_ref[...]) + sh_ref[...]).astype(h_ref.dtype)


def pre_call(x, mod4):
    tm = 512
    return pl.pallas_call(
        _pre_kernel,
        out_shape=jax.ShapeDtypeStruct((NT, D_MODEL), jnp.bfloat16),
        grid=(NT // tm,),
        in_specs=[pl.BlockSpec((tm, D_MODEL), lambda i: (i, 0)),
                  _mod_spec(0, 0, tm), _mod_spec(0, 1, tm)],
        out_specs=pl.BlockSpec((tm, D_MODEL), lambda i: (i, 0)),
        compiler_params=_cparams(1),
        name="pre_modulate",
    )(x, mod4, mod4)


def _proj_kernel(x_ref, w_ref, o_ref, wb_ref, *, act):
    @pl.when(pl.program_id(1) == 0)
    def _():
        wb_ref[...] = w_ref[...].astype(jnp.bfloat16)

    acc = jnp.dot(x_ref[...], wb_ref[...], preferred_element_type=jnp.float32)
    if act == "sigmoid":
        acc = jax.nn.sigmoid(acc)
    o_ref[...] = acc.astype(o_ref.dtype)


def proj_call(h, w_in, l, col_off, width, act, out_dtype, name):
    tm, tn = 1024, 512
    off = col_off // tn
    return pl.pallas_call(
        functools.partial(_proj_kernel, act=act),
        out_shape=jax.ShapeDtypeStruct((NT, width), out_dtype),
        grid=(width // tn, NT // tm),
        in_specs=[pl.BlockSpec((tm, D_MODEL), lambda j, i: (i, 0)),
                  pl.BlockSpec((None, D_MODEL, tn), lambda j, i: (l, 0, off + j))],
        out_specs=pl.BlockSpec((tm, tn), lambda j, i: (i, j)),
        scratch_shapes=[pltpu.VMEM((D_MODEL, tn), jnp.bfloat16)],
        compiler_params=_cparams(2),
        name=name,
    )(h, w_in)


def _swap32(x):
    lane = lax.broadcasted_iota(jnp.int32, x.shape, 1)
    up = pltpu.roll(x, LANE - 32, 1)
    dn = pltpu.roll(x, 32, 1)
    return jnp.where((lane & 63) < 32, up, dn)


def _qkv_kernel(x_ref, w_ref, cos_ref, sin_ref, o_ref, wb_ref):
    @pl.when(pl.program_id(0) == 0)
    def _():
        wb_ref[...] = w_ref[...].astype(jnp.bfloat16)

    cos = cos_ref[...]
    sin = sin_ref[...]
    scale = HEAD_DIM ** -0.5
    for hd in range(QKV_WIDTH // HEAD_DIM):
        sl = slice(hd * HEAD_DIM, (hd + 1) * HEAD_DIM)
        acc = jnp.dot(x_ref[...], wb_ref[:, sl], preferred_element_type=jnp.float32)
        if hd < N_HEADS + N_KV_HEADS:
            acc = acc * cos + _swap32(acc) * sin
        if hd < N_HEADS:
            acc = acc * scale
        o_ref[:, sl] = acc.astype(o_ref.dtype)


def qkv_call(h, w_in, l, cos_t, sin_t):
    tm = 512
    n_pos = SEQ // tm

    def rope_map(i):
        return (jnp.where(i * tm < N_LAT, i % n_pos, n_pos), 0)

    return pl.pallas_call(
        _qkv_kernel,
        out_shape=jax.ShapeDtypeStruct((NT, QKV_WIDTH), jnp.bfloat16),
        grid=(NT // tm,),
        in_specs=[pl.BlockSpec((tm, D_MODEL), lambda i: (i, 0)),
                  pl.BlockSpec((None, D_MODEL, QKV_WIDTH), lambda i: (l, 0, 0),
                               pipeline_mode=pl.Buffered(1)),
                  pl.BlockSpec((tm, HEAD_DIM), rope_map),
                  pl.BlockSpec((tm, HEAD_DIM), rope_map)],
        out_specs=pl.BlockSpec((tm, QKV_WIDTH), lambda i: (i, 0)),
        scratch_shapes=[pltpu.VMEM((D_MODEL, QKV_WIDTH), jnp.bfloat16)],
        compiler_params=_cparams(1),
        name="qkv_rope",
    )(h, w_in, cos_t, sin_t)


def rope_tables(tm):
    n_freq = HEAD_DIM // 4
    inv_freq = ROPE_THETA ** (-np.arange(n_freq, dtype=np.float64) / n_freq)
    t = np.arange(SEQ)
    ang_row = (t // GRID_W)[:, None] * inv_freq[None, :]
    ang_col = (t % GRID_W)[:, None] * inv_freq[None, :]
    cos = np.concatenate([np.cos(ang_row), np.cos(ang_row), np.cos(ang_col), np.cos(ang_col)], 1)
    sin = np.concatenate([-np.sin(ang_row), np.sin(ang_row), -np.sin(ang_col), np.sin(ang_col)], 1)
    cos = np.concatenate([cos, np.ones((tm, HEAD_DIM))], 0)
    sin = np.concatenate([sin, np.zeros((tm, HEAD_DIM))], 0)
    return jnp.asarray(cos, jnp.float32), jnp.asarray(sin, jnp.float32)


ATT_QT = 256
ATT_LAT_TILES = N_LAT // ATT_QT
ATT_TILES_PER_SEQ = SEQ // ATT_QT


def _attn_kernel(sink_ref, q_ref, kp_ref, kc_ref, kn_ref, vp_ref, vc_ref, vn_ref,
                 kx_ref, vx_ref, o_ref, *, l):
    i = pl.program_id(0)
    h = pl.program_id(1)
    is_lat = i < ATT_LAT_TILES
    base = (i % ATT_TILES_PER_SEQ) * ATT_QT
    n_loc = ATT_QT + 2 * WINDOW
    r = lax.broadcasted_iota(jnp.int32, (ATT_QT, n_loc), 0)
    j = lax.broadcasted_iota(jnp.int32, (ATT_QT, n_loc), 1)
    kpos = base - WINDOW + j
    dist = j - r
    k_end = jnp.where(is_lat, SEQ, 0)
    valid = (dist >= 0) & (dist <= 2 * WINDOW) & (kpos >= 0) & (kpos < k_end)

    k_loc = jnp.concatenate([kp_ref[...], kc_ref[...], kn_ref[...]], axis=0)
    v_loc = jnp.concatenate([vp_ref[...], vc_ref[...], vn_ref[...]], axis=0)
    k_ctx = kx_ref[...]
    v_ctx = vx_ref[...]
    dn = (((1,), (1,)), ((), ()))
    for g in range(Q_GROUP):
        sl = slice(g * HEAD_DIM, (g + 1) * HEAD_DIM)
        q = q_ref[:, sl]
        s_loc = lax.dot_general(q, k_loc, dn, preferred_element_type=jnp.float32)
        s_loc = jnp.where(valid, s_loc, MASK_VALUE)
        s_ctx = lax.dot_general(q, k_ctx, dn, preferred_element_type=jnp.float32)
        sink = sink_ref[l, h * Q_GROUP + g]
        m = jnp.maximum(jnp.maximum(jnp.max(s_loc, -1, keepdims=True),
                                    jnp.max(s_ctx, -1, keepdims=True)), sink)
        e_loc = jnp.exp(s_loc - m)
        e_ctx = jnp.exp(s_ctx - m)
        den = (jnp.sum(e_loc, -1, keepdims=True) + jnp.sum(e_ctx, -1, keepdims=True)
               + jnp.exp(sink - m))
        o = (jnp.dot(e_loc.astype(jnp.bfloat16), v_loc, preferred_element_type=jnp.float32)
             + jnp.dot(e_ctx.astype(jnp.bfloat16), v_ctx, preferred_element_type=jnp.float32))
        o_ref[:, sl] = (o / den).astype(o_ref.dtype)


def attn_call(qkv, sink, l):
    blk = WINDOW
    per = ATT_QT // blk
    last_blk = NT // blk - 1
    kcol, vcol = K_OFF // HEAD_DIM, V_OFF // HEAD_DIM

    def batch_of(i):
        return jnp.where(i < ATT_LAT_TILES, i // ATT_TILES_PER_SEQ, i - ATT_LAT_TILES)

    def prev_map(col):
        return lambda i, h: (jnp.maximum(i * per - 1, 0), col + h)

    def next_map(col):
        return lambda i, h: (jnp.minimum(i * per + per, last_blk), col + h)

    def cur_map(col):
        return lambda i, h: (i, col + h)

    def ctx_map(col):
        return lambda i, h: (N_LAT // CTX_LEN + batch_of(i), col + h)

    return pl.pallas_call(
        functools.partial(_attn_kernel, l=l),
        out_shape=jax.ShapeDtypeStruct((NT, ATTN_WIDTH), jnp.bfloat16),
        grid=(NT // ATT_QT, N_KV_HEADS),
        in_specs=[pl.BlockSpec(memory_space=pltpu.SMEM),
                  pl.BlockSpec((ATT_QT, Q_GROUP * HEAD_DIM), lambda i, h: (i, h)),
                  pl.BlockSpec((blk, HEAD_DIM), prev_map(kcol)),
                  pl.BlockSpec((ATT_QT, HEAD_DIM), cur_map(kcol)),
                  pl.BlockSpec((blk, HEAD_DIM), next_map(kcol)),
                  pl.BlockSpec((blk, HEAD_DIM), prev_map(vcol)),
                  pl.BlockSpec((ATT_QT, HEAD_DIM), cur_map(vcol)),
                  pl.BlockSpec((blk, HEAD_DIM), next_map(vcol)),
                  pl.BlockSpec((CTX_LEN, HEAD_DIM), ctx_map(kcol)),
                  pl.BlockSpec((CTX_LEN, HEAD_DIM), ctx_map(vcol))],
        out_specs=pl.BlockSpec((ATT_QT, Q_GROUP * HEAD_DIM), lambda i, h: (i, h)),
        compiler_params=_cparams(2),
        name="window_attention",
    )(sink, qkv, qkv, qkv, qkv, qkv, qkv, qkv, qkv, qkv)


def _mix_kernel(a_ref, yh_ref, yf_ref, g_ref, x_ref, wa_ref, wh_ref, wf_ref, wo_ref,
                g1_ref, sh2_ref, sc2_ref, lg_ref, lb_ref, xo_ref, t_ref, y_ref):
    tn = 512
    for c in range(D_MODEL // tn):
        sl = slice(c * tn, (c + 1) * tn)
        ya = jnp.dot(a_ref[...], wa_ref[:, sl], preferred_element_type=jnp.float32)
        yh = jnp.dot(yh_ref[...], wh_ref[:, sl], preferred_element_type=jnp.float32)
        yf = jnp.dot(yf_ref[...], wf_ref[:, sl], preferred_element_type=jnp.float32)
        y = (g_ref[:, sl].astype(jnp.float32) * ya
             + g_ref[:, D_MODEL + c * tn:D_MODEL + (c + 1) * tn].astype(jnp.float32) * yh
             + g_ref[:, 2 * D_MODEL + c * tn:2 * D_MODEL + (c + 1) * tn].astype(jnp.float32) * yf)
        y_ref[:, sl] = y.astype(y_ref.dtype)
    o = jnp.dot(y_ref[...], wo_ref[...], preferred_element_type=jnp.float32)
    xn = _ln(DEEPNORM_ALPHA * x_ref[...] + g1_ref[...] * o) * lg_ref[...] + lb_ref[...]
    xo_ref[...] = xn
    t_ref[...] = (_ln(xn) * (1.0 + sc2_ref[...]) + sh2_ref[...]).astype(t_ref.dtype)


def mix_call(a, yh, yf, gates, x, wa, wh, wf, wo, mod4, ln_g, ln_b, l):
    tm = 256
    row = lambda i: (i, 0)
    full = lambda i: (0, 0)
    const = pl.Buffered(1)
    return pl.pallas_call(
        _mix_kernel,
        out_shape=(jax.ShapeDtypeStruct((NT, D_MODEL), jnp.float32),
                   jax.ShapeDtypeStruct((NT, D_MODEL), jnp.float32)),
        grid=(NT // tm,),
        in_specs=[pl.BlockSpec((tm, ATTN_WIDTH), row),
                  pl.BlockSpec((tm, HYENA_WIDTH), row),
                  pl.BlockSpec((tm, FNET_WIDTH), row),
                  pl.BlockSpec((tm, 3 * D_MODEL), row),
                  pl.BlockSpec((tm, D_MODEL), row),
                  pl.BlockSpec((ATTN_WIDTH, D_MODEL), full, pipeline_mode=const),
                  pl.BlockSpec((HYENA_WIDTH, D_MODEL), full, pipeline_mode=const),
                  pl.BlockSpec((FNET_WIDTH, D_MODEL), full, pipeline_mode=const),
                  pl.BlockSpec((D_MODEL, D_MODEL), full, pipeline_mode=const),
                  _mod_spec(l, 2, tm), _mod_spec(l, 3, tm), _mod_spec(l, 4, tm),
                  pl.BlockSpec((None, 1, D_MODEL), lambda i: (l, 0, 0)),
                  pl.BlockSpec((None, 1, D_MODEL), lambda i: (l, 0, 0))],
        out_specs=(pl.BlockSpec((tm, D_MODEL), row), pl.BlockSpec((tm, D_MODEL), row)),
        scratch_shapes=[pltpu.VMEM((tm, D_MODEL), jnp.bfloat16)],
        compiler_params=_cparams(1),
        name="mixer_out",
    )(a, yh, yf, gates, x, wa, wh, wf, wo, mod4, mod4, mod4, ln_g, ln_b)


ROUTE_LANES = LANE


def _route_kernel(t_ref, w_ref, b_ref, o_ref):
    logits = jnp.dot(t_ref[...].astype(jnp.bfloat16), w_ref[...],
                     preferred_element_type=jnp.float32) + b_ref[...]
    lane = lax.broadcasted_iota(jnp.int32, logits.shape, 1)
    neg = -jnp.inf
    big = jnp.int32(2 ** 30)

    def first_argmax(vals):
        mx = jnp.max(vals, -1, keepdims=True)
        idx = jnp.min(jnp.where(vals == mx, lane, big), -1, keepdims=True)
        return mx, idx

    g_vals = jnp.where(lane < N_GROUPS, logits, neg)
    g_max, grp = first_argmax(g_vals)
    g_w = 1.0 / jnp.sum(jnp.exp(g_vals - g_max), -1, keepdims=True)
    lo = N_GROUPS + grp * EXPERTS_PER_GROUP
    e_vals = jnp.where((lane >= lo) & (lane < lo + EXPERTS_PER_GROUP), logits, neg)
    v1, i1 = first_argmax(e_vals)
    v2, i2 = first_argmax(jnp.where(lane == i1, neg, e_vals))
    e2 = jnp.exp(v2 - v1)
    w1 = g_w / (1.0 + e2)
    w2 = g_w * e2 / (1.0 + e2)
    id1 = (i1 - N_GROUPS).astype(jnp.float32)
    id2 = (i2 - N_GROUPS).astype(jnp.float32)
    o_ref[...] = jnp.where(lane == 0, id1, jnp.where(lane == 1, id2,
                           jnp.where(lane == 2, w1, jnp.where(lane == 3, w2, 0.0))))


def route_call(t, wr, br):
    tm = 512
    return pl.pallas_call(
        _route_kernel,
        out_shape=jax.ShapeDtypeStruct((NT, ROUTE_LANES), jnp.float32),
        grid=(NT // tm,),
        in_specs=[pl.BlockSpec((tm, D_MODEL), lambda i: (i, 0)),
                  pl.BlockSpec((D_MODEL, ROUTE_LANES), lambda i: (0, 0)),
                  pl.BlockSpec((1, ROUTE_LANES), lambda i: (0, 0))],
        out_specs=pl.BlockSpec((tm, ROUTE_LANES), lambda i: (i, 0)),
        compiler_params=_cparams(1),
        name="moe_router",
    )(t, wr, br)


MOE_OUT_ROWS = MOE_SLOTS


def _moe_kernel(be_ref, nu_ref, gi_ref, gn_ref, si_ref, t_hbm, w1_ref, w3_ref, w2_ref, o_hbm,
                b1_ref, b3_ref, b2_ref, xbuf, obuf, gsem, ssem):
    i = pl.program_id(0)
    slot = i % 2

    def gather(idx_ref, s):
        for r in range(MOE_TB):
            pltpu.make_async_copy(t_hbm.at[pl.ds(idx_ref[0, r], 1)], xbuf.at[s, pl.ds(r, 1)],
                                  gsem.at[s]).start()

    def wait_rows(src, dst, sem):
        pltpu.make_async_copy(src, dst, sem).wait()

    @pl.when(i == 0)
    def _():
        gather(gi_ref, 0)

    wait_rows(t_hbm.at[pl.ds(0, MOE_TB)], xbuf.at[slot], gsem.at[slot])

    @pl.when(i + 1 < MOE_BLOCKS)
    def _():
        gather(gn_ref, 1 - slot)

    first = (i == 0) | (be_ref[i] != be_ref[jnp.maximum(i - 1, 0)])

    @pl.when(first)
    def _():
        b1_ref[...] = w1_ref[...].astype(jnp.bfloat16)
        b3_ref[...] = w3_ref[...].astype(jnp.bfloat16)
        b2_ref[...] = w2_ref[...].astype(jnp.bfloat16)

    @pl.when(i >= 2)
    def _():
        wait_rows(obuf.at[slot], o_hbm.at[pl.ds(0, MOE_TB)], ssem.at[slot])

    @pl.when(i < nu_ref[0])
    def _():
        x = xbuf[slot].astype(jnp.bfloat16)
        h1 = jnp.dot(x, b1_ref[...], preferred_element_type=jnp.float32)
        h3 = jnp.dot(x, b3_ref[...], preferred_element_type=jnp.float32)
        hh = (h1 * jax.nn.sigmoid(h1) * h3).astype(jnp.bfloat16)
        obuf[slot] = jnp.dot(hh, b2_ref[...], preferred_element_type=jnp.float32)

    @pl.when(i >= nu_ref[0])
    def _():
        obuf[slot] = jnp.zeros(obuf.shape[1:], obuf.dtype)

    for r in range(MOE_TB):
        pltpu.make_async_copy(obuf.at[slot, pl.ds(r, 1)], o_hbm.at[pl.ds(si_ref[0, r], 1)],
                              ssem.at[slot]).start()

    @pl.when(i == MOE_BLOCKS - 1)
    def _():
        wait_rows(obuf.at[1 - slot], o_hbm.at[pl.ds(0, MOE_TB)], ssem.at[1 - slot])
        wait_rows(obuf.at[slot], o_hbm.at[pl.ds(0, MOE_TB)], ssem.at[slot])


def moe_call(block_e, n_used, slot_tok, slot_dst, t, w1, w3, w2, l):
    smem_blk = lambda f: pl.BlockSpec((None, 1, MOE_TB), f, memory_space=pltpu.SMEM)
    grid_spec = pltpu.PrefetchScalarGridSpec(
        num_scalar_prefetch=2,
        grid=(MOE_BLOCKS,),
        in_specs=[smem_blk(lambda i, be, nu: (i, 0, 0)),
                  smem_blk(lambda i, be, nu: (jnp.minimum(i + 1, MOE_BLOCKS - 1), 0, 0)),
                  smem_blk(lambda i, be, nu: (i, 0, 0)),
                  pl.BlockSpec(memory_space=pl.ANY),
                  pl.BlockSpec((None, None, D_MODEL, EXPERT_HIDDEN), lambda i, be, nu: (l, be[i], 0, 0)),
                  pl.BlockSpec((None, None, D_MODEL, EXPERT_HIDDEN), lambda i, be, nu: (l, be[i], 0, 0)),
                  pl.BlockSpec((None, None, EXPERT_HIDDEN, D_MODEL), lambda i, be, nu: (l, be[i], 0, 0))],
        out_specs=pl.BlockSpec(memory_space=pl.ANY),
        scratch_shapes=[pltpu.VMEM((D_MODEL, EXPERT_HIDDEN), jnp.bfloat16),
                        pltpu.VMEM((D_MODEL, EXPERT_HIDDEN), jnp.bfloat16),
                        pltpu.VMEM((EXPERT_HIDDEN, D_MODEL), jnp.bfloat16),
                        pltpu.VMEM((2, MOE_TB, D_MODEL), jnp.float32),
                        pltpu.VMEM((2, MOE_TB, D_MODEL), jnp.float32),
                        pltpu.SemaphoreType.DMA((2,)),
                        pltpu.SemaphoreType.DMA((2,))])
    return pl.pallas_call(
        _moe_kernel,
        out_shape=jax.ShapeDtypeStruct((MOE_OUT_ROWS, D_MODEL), jnp.float32),
        grid_spec=grid_spec,
        compiler_params=pltpu.CompilerParams(dimension_semantics=("arbitrary",),
                                             vmem_limit_bytes=VMEM_LIMIT, has_side_effects=True),
        name="moe_experts",
    )(block_e, n_used, slot_tok, slot_tok, slot_dst, t, w1, w3, w2)


def _post_kernel(x_ref, y0_ref, y1_ref, r_ref, g2_ref, lg_ref, lb_ref, sh_ref, sc_ref,
                 xo_ref, h_ref):
    r = r_ref[...]
    m = r[:, 2:3] * y0_ref[...] + r[:, 3:4] * y1_ref[...]
    xn = _ln(DEEPNORM_ALPHA * x_ref[...] + g2_ref[...] * m) * lg_ref[...] + lb_ref[...]
    xo_ref[...] = xn
    h_ref[...] = (_ln(xn) * (1.0 + sc_ref[...]) + sh_ref[...]).astype(h_ref.dtype)


def post_call(x, yk, route, mod4, ln_g, ln_b, l):
    tm = 512
    ln = min(l + 1, DEPTH - 1)
    row = lambda i: (i, 0)
    return pl.pallas_call(
        _post_kernel,
        out_shape=(jax.ShapeDtypeStruct((NT, D_MODEL), jnp.float32),
                   jax.ShapeDtypeStruct((NT, D_MODEL), jnp.bfloat16)),
        grid=(NT // tm,),
        in_specs=[pl.BlockSpec((tm, D_MODEL), row),
                  pl.BlockSpec((tm, D_MODEL), row),
                  pl.BlockSpec((tm, D_MODEL), lambda i: (NT // tm + i, 0)),
                  pl.BlockSpec((tm, ROUTE_LANES), row),
                  _mod_spec(l, 5, tm),
                  pl.BlockSpec((None, 1, D_MODEL), lambda i: (l, 0, 0)),
                  pl.BlockSpec((None, 1, D_MODEL), lambda i: (l, 0, 0)),
                  _mod_spec(ln, 0, tm), _mod_spec(ln, 1, tm)],
        out_specs=(pl.BlockSpec((tm, D_MODEL), row), pl.BlockSpec((tm, D_MODEL), row)),
        compiler_params=_cparams(1),
        name="moe_combine_postln",
    )(x, yk, yk, route, mod4, ln_g, ln_b, mod4, mod4)


CH = HYENA_WIDTH


def _hi_lo(x):
    hi = x.astype(jnp.bfloat16)
    lo = (x - hi.astype(jnp.float32)).astype(jnp.bfloat16)
    return hi, lo


def _const_lhs3(m):
    hi, lo = _hi_lo(jnp.asarray(m, jnp.float32))
    return jnp.concatenate([hi, lo, hi], 1)


def _data_rhs3(x):
    hi, lo = _hi_lo(x)
    return jnp.concatenate([hi, hi, lo], 0)


def _const_rhs3(m):
    hi, lo = _hi_lo(jnp.asarray(m, jnp.float32))
    return jnp.concatenate([hi, hi, lo], 0)


def _data_lhs3(x):
    hi, lo = _hi_lo(x)
    return jnp.concatenate([hi, lo, hi], 1)


def _cs(n, rows, cols):
    ang = 2.0 * np.pi * np.outer(np.arange(rows), np.arange(cols)) / n
    return np.cos(ang), np.sin(ang)


def dft_consts(n_fft, n1):
    n2 = n_fft // n1
    c1, s1 = _cs(n1, n1, n1)
    c2, s2 = _cs(n2, n2, n2)
    h = n1 // 2
    tr, ti = _cs(n_fft, n1, n2)
    return dict(
        n1=n1, n2=n2,
        lead_real=_const_lhs3(np.concatenate([c1, -s1], 0)),
        lead_pair=_const_lhs3(np.block([[c1[:, :h], s1[:, :h]], [-s1[:, :h], c1[:, :h]]])),
        lead_inv=_const_lhs3(np.block([[c1[:h, :], -s1[:h, :]], [s1[:h, :], c1[:h, :]]]) / n_fft),
        f2=_const_lhs3(np.block([[c2, s2], [-s2, c2]])),
        f2_inv=_const_lhs3(np.block([[c2, -s2], [s2, c2]])),
        tw_r=jnp.asarray(tr[:, :, None], jnp.float32),
        tw_i=jnp.asarray(-ti[:, :, None], jnp.float32),
    )


def _lead_kernel(m_ref, x_ref, o_ref):
    o_ref[...] = jnp.dot(m_ref[...], _data_rhs3(x_ref[...]), preferred_element_type=jnp.float32)


def lead_call(mcat, x2d, k, row_off, groups, name):
    mo = mcat.shape[0]
    width = x2d.shape[1]
    tn = min(width, 8192)
    off = row_off // k
    return pl.pallas_call(
        _lead_kernel,
        out_shape=jax.ShapeDtypeStruct((groups, mo, width), jnp.float32),
        grid=(groups, width // tn),
        in_specs=[pl.BlockSpec(mcat.shape, lambda g, j: (0, 0)),
                  pl.BlockSpec((k, tn), lambda g, j: (off + g, j))],
        out_specs=pl.BlockSpec((None, mo, tn), lambda g, j: (g, 0, j)),
        compiler_params=_cparams(2),
        name=name,
    )(mcat, x2d)


def _lead_out_kernel(m_ref, d_ref, z_ref, x0_ref, db_ref, s_ref, o_ref):
    acc = jnp.dot(m_ref[...], _data_rhs3(d_ref[...]), preferred_element_type=jnp.float32)
    y = acc * (1.0 / (s_ref[...] + 1e-6)) + z_ref[...] * db_ref[...]
    o_ref[...] = (x0_ref[...] * y).astype(o_ref.dtype)


def lead_out_call(mcat, d3, z2d, x02d, dbias_t, ssum_t, k_out, row_off, name):
    groups, k_in, width = d3.shape
    tn = min(width, 8192)
    off = row_off // k_out
    view = pl.BlockSpec((k_out, tn), lambda g, j: (off + g, j))
    lanes = pl.BlockSpec((1, tn), lambda g, j: (0, 0))
    return pl.pallas_call(
        _lead_out_kernel,
        out_shape=jax.ShapeDtypeStruct((groups, k_out, width), jnp.bfloat16),
        grid=(groups, width // tn),
        in_specs=[pl.BlockSpec(mcat.shape, lambda g, j: (0, 0)),
                  pl.BlockSpec((None, k_in, tn), lambda g, j: (g, 0, j)),
                  view, view, lanes, lanes],
        out_specs=pl.BlockSpec((None, k_out, tn), lambda g, j: (g, 0, j)),
        compiler_params=_cparams(2),
        name=name,
    )(mcat, d3, z2d, x02d, dbias_t, ssum_t)


def _hy_slab_kernel(*refs, kb, n2, conv):
    if conv:
        ar_ref, ai_ref, tr_ref, ti_ref, f_ref, fi_ref, hr_ref, hi_ref, o_ref = refs
    else:
        ar_ref, ai_ref, tr_ref, ti_ref, f_ref, o_ref = refs
    f = f_ref[...]
    for j in range(kb):
        ar, ai, tr, ti = ar_ref[j], ai_ref[j], tr_ref[j], ti_ref[j]
        br = ar * tr - ai * ti
        bi = ar * ti + ai * tr
        x = jnp.dot(f, _data_rhs3(jnp.concatenate([br, bi], 0)), preferred_element_type=jnp.float32)
        xr, xi = x[:n2], x[n2:]
        if conv:
            hr, hi = hr_ref[j], hi_ref[j]
            yr = xr * hr - xi * hi
            yi = xr * hi + xi * hr
            d = jnp.dot(fi_ref[...], _data_rhs3(jnp.concatenate([yr, yi], 0)),
                        preferred_element_type=jnp.float32)
            dr, di = d[:n2], d[n2:]
            o_ref[0, j] = dr * tr + di * ti
            o_ref[1, j] = di * tr - dr * ti
        else:
            o_ref[0, j] = xr
            o_ref[1, j] = xi


def hy_slab_call(a3, cst, h5, name):
    n1, n2 = cst["n1"], cst["n2"]
    groups = a3.shape[0]
    kb = 8
    a5 = a3.reshape(groups, 2, n1, n2, CH)
    slab = lambda part: pl.BlockSpec((None, None, kb, n2, CH), lambda b, g: (g, part, b, 0, 0))
    tw = pl.BlockSpec((kb, n2, 1), lambda b, g: (b, 0, 0))
    mat = pl.BlockSpec(cst["f2"].shape, lambda b, g: (0, 0))
    in_specs = [slab(0), slab(1), tw, tw, mat]
    args = [a5, a5, cst["tw_r"], cst["tw_i"], cst["f2"]]
    conv = h5 is not None
    if conv:
        hs = lambda part: pl.BlockSpec((None, None, kb, n2, CH), lambda b, g: (0, part, b, 0, 0))
        in_specs += [mat, hs(0), hs(1)]
        args += [cst["f2_inv"], h5, h5]
    return pl.pallas_call(
        functools.partial(_hy_slab_kernel, kb=kb, n2=n2, conv=conv),
        out_shape=jax.ShapeDtypeStruct((groups, 2, n1, n2, CH), jnp.float32),
        grid=(n1 // kb, groups),
        in_specs=in_specs,
        out_specs=pl.BlockSpec((None, 2, kb, n2, CH), lambda b, g: (g, 0, b, 0, 0)),
        compiler_params=_cparams(2),
        name=name,
    )(*args)


def filter_tables(n):
    m = np.arange(2 * n)
    pos = np.where(m < n, m, 2 * n - m)
    pos = np.where(m == n, 0, pos)
    t = pos / (n - 1.0)
    w = 2.0 * np.pi * pos / n
    bands = np.linspace(1e-4, HYENA_BANDS - 1, HYENA_BANDS)
    feat = np.zeros((2 * n, LANE))
    feat[:, 0] = t
    feat[:, 1:1 + HYENA_BANDS] = np.cos(bands[None, :] * w[:, None])
    feat[:, 1 + HYENA_BANDS:HYENA_EMB] = -np.sin(bands[None, :] * w[:, None])
    aux = np.zeros((2 * n, LANE))
    aux[:, 0] = t
    aux[:, 1] = (m < n)
    aux[:, 2] = (m > n)
    deltas = np.abs(np.linspace(HYENA_MIN_DECAY, HYENA_MAX_DECAY, HYENA_WIDTH))[None, :]
    return (jnp.asarray(feat, jnp.float32), jnp.asarray(aux, jnp.float32),
            jnp.asarray(deltas, jnp.float32))


def _filt_kernel(z_ref, aux_ref, dl_ref, w1_ref, b1_ref, w2_ref, b2_ref, w3_ref, b3_ref,
                 w4_ref, b4_ref, fr_ref, o_ref, s_ref):
    hp = lax.Precision.HIGHEST
    f32 = jnp.float32
    fr = fr_ref[...]
    h = jnp.sin(fr[0:1] * (jnp.dot(z_ref[...], w1_ref[...], precision=hp, preferred_element_type=f32)
                           + b1_ref[...]))
    h = jnp.sin(fr[1:2] * (jnp.dot(h, w2_ref[...], precision=hp, preferred_element_type=f32)
                           + b2_ref[...]))
    h = jnp.sin(fr[2:3] * (jnp.dot(h, w3_ref[...], precision=hp, preferred_element_type=f32)
                           + b3_ref[...]))
    f = jnp.dot(h, w4_ref[...], precision=hp, preferred_element_type=f32) + b4_ref[...]
    aux = aux_ref[...]
    dec = jnp.exp(-aux[:, 0:1] * dl_ref[...])
    c = (f[:, :HYENA_WIDTH] * aux[:, 1:2] + f[:, HYENA_WIDTH:] * aux[:, 2:3]) * dec
    o_ref[...] = c

    @pl.when(pl.program_id(0) == 0)
    def _():
        s_ref[...] = jnp.zeros_like(s_ref)

    s_ref[...] += jnp.sum(jnp.abs(c), 0, keepdims=True)


def filter_call(n, tabs, w1p, b1, w2, b2, w3, b3, w4, b4, freq):
    feat, aux, deltas = tabs
    tr = min(2 * n, 1024)
    row = lambda i: (i, 0)
    full = lambda a: pl.BlockSpec(a.shape, lambda i: (0,) * a.ndim)
    ws = [w1p, b1, w2, b2, w3, b3, w4, b4, freq]
    return pl.pallas_call(
        _filt_kernel,
        out_shape=(jax.ShapeDtypeStruct((2 * n, HYENA_WIDTH), jnp.float32),
                   jax.ShapeDtypeStruct((1, HYENA_WIDTH), jnp.float32)),
        grid=(2 * n // tr,),
        in_specs=[pl.BlockSpec((tr, LANE), row), pl.BlockSpec((tr, LANE), row), full(deltas)]
                 + [full(a) for a in ws],
        out_specs=(pl.BlockSpec((tr, HYENA_WIDTH), row),
                   pl.BlockSpec((1, HYENA_WIDTH), lambda i: (0, 0))),
        compiler_params=_cparams(1),
        name="hyena_filter",
    )(feat, aux, deltas, *ws)


def fnet_consts(n, n1):
    cst = dft_consts(n, n1)
    c, s = _cs(FNET_GROUP_DIM, FNET_GROUP_DIM, FNET_GROUP_DIM)
    cst["chan"] = _const_rhs3(np.concatenate([c, s], 0) / math.sqrt(n * FNET_GROUP_DIM))
    return cst


def _fn_slab_kernel(ar_ref, ai_ref, tr_ref, ti_ref, f_ref, cs_ref, o_ref, *, kb, n2):
    f = f_ref[...]
    cs = cs_ref[...]
    gd = FNET_GROUP_DIM
    for j in range(kb):
        ar, ai, tr, ti = ar_ref[j], ai_ref[j], tr_ref[j], ti_ref[j]
        br = ar * tr - ai * ti
        bi = ar * ti + ai * tr
        v = jnp.dot(f, _data_rhs3(jnp.concatenate([br, bi], 0)), preferred_element_type=jnp.float32)
        vr, vi = v[:n2], v[n2:]
        for g in range(FNET_GROUPS):
            lhs = jnp.concatenate([vr[:, g * gd:(g + 1) * gd], vi[:, g * gd:(g + 1) * gd]], 1)
            y = jnp.dot(_data_lhs3(lhs), cs, preferred_element_type=jnp.float32)
            o_ref[:, j * CH + g * gd:j * CH + (g + 1) * gd] = y.astype(o_ref.dtype)


def fn_slab_call(a3, cst, name):
    n1, n2 = cst["n1"], cst["n2"]
    groups = a3.shape[0]
    kb = 8
    a5 = a3.reshape(groups, 2, n1, n2, CH)
    slab = lambda part: pl.BlockSpec((None, None, kb, n2, CH), lambda b, g: (g, part, b, 0, 0))
    tw = pl.BlockSpec((kb, n2, 1), lambda b, g: (b, 0, 0))
    return pl.pallas_call(
        functools.partial(_fn_slab_kernel, kb=kb, n2=n2),
        out_shape=jax.ShapeDtypeStruct((groups, n2, n1 * CH), jnp.bfloat16),
        grid=(n1 // kb, groups),
        in_specs=[slab(0), slab(1), tw, tw,
                  pl.BlockSpec(cst["f2"].shape, lambda b, g: (0, 0)),
                  pl.BlockSpec(cst["chan"].shape, lambda b, g: (0, 0))],
        out_specs=pl.BlockSpec((None, n2, kb * CH), lambda b, g: (g, 0, b)),
        compiler_params=_cparams(2),
        name=name,
    )(a5, a5, cst["tw_r"], cst["tw_i"], cst["f2"], cst["chan"])


def hyena_long(z, x0, n, batch, row_off, cst, tabs, filt_params, d_bias, tag):
    n1, n2 = cst["n1"], cst["n2"]
    w1, b1, w2, b2, w3, b3, w4, b4, freq = filt_params
    w1p = jnp.concatenate([w1, jnp.zeros((LANE - HYENA_EMB, HYENA_FILTER_HIDDEN), w1.dtype)], 0)
    circ, ssum = filter_call(n, tabs, w1p, b1[None, :], w2, b2[None, :], w3, b3[None, :],
                             w4, b4[None, :], freq)
    width = n2 * CH
    h3 = lead_call(cst["lead_real"], circ.reshape(n1, width), n1, 0, 1, "hyena_filt_dft1_" + tag)
    h5 = hy_slab_call(h3, cst, None, "hyena_filt_dft2_" + tag)
    z2d = z.reshape(-1, width)
    x02d = x0.reshape(-1, width)
    pairs = batch // 2
    a3 = lead_call(cst["lead_pair"], z2d, n1, row_off // n2, pairs, "hyena_dft1_" + tag)
    d5 = hy_slab_call(a3, cst, h5, "hyena_conv_" + tag)
    reps = width // CH if width <= 8192 else 8192 // CH
    y = lead_out_call(cst["lead_inv"], d5.reshape(pairs, 2 * n1, width), z2d, x02d,
                      jnp.tile(d_bias[None, :], (1, reps)), jnp.tile(ssum, (1, reps)),
                      n1, row_off // n2, "hyena_idft1_" + tag)
    return y.reshape(batch * n, CH)


def fnet_seq(u, n, batch, row_off, cst, tag):
    n1, n2 = cst["n1"], cst["n2"]
    width = n2 * CH
    a3 = lead_call(cst["lead_real"], u.reshape(-1, width), n1, row_off // n2, batch, "fnet_dft1_" + tag)
    return fn_slab_call(a3, cst, "fnet_dft2_" + tag).reshape(batch * n, CH)


SC_TM = CTX_LEN


def _sconv_kernel(u_ref, p_ref, n_ref, w_ref, b_ref, z_ref, x0_ref, *, lat_tiles, tiles_per_seq):
    i = pl.program_id(0)
    pos = i % tiles_per_seq
    is_lat = i < lat_tiles
    keep_prev = jnp.where(is_lat & (pos != 0), 1.0, 0.0)
    keep_next = jnp.where(is_lat & (pos != tiles_per_seq - 1), 1.0, 0.0)
    u = u_ref[...]
    tm = u.shape[0]
    row = lax.broadcasted_iota(jnp.int32, u.shape, 0)
    um = jnp.where(row == 0, p_ref[7:8, :] * keep_prev, pltpu.roll(u, 1, 0))
    up = jnp.where(row == tm - 1, n_ref[0:1, :] * keep_next, pltpu.roll(u, tm - 1, 0))
    w = w_ref[...]
    uc = b_ref[...] + um * w[0:1] + u * w[1:2] + up * w[2:3]
    x0_ref[...] = uc[:, :CH]
    z_ref[...] = uc[:, CH:2 * CH] * uc[:, 2 * CH:]


def sconv_call(u, conv_w, conv_b, l, n_lat, seq):
    rows = u.shape[0]
    tm = SC_TM
    per8 = tm // 8
    last8 = rows // 8 - 1
    out = jax.ShapeDtypeStruct((rows, CH), jnp.float32)
    return pl.pallas_call(
        functools.partial(_sconv_kernel, lat_tiles=n_lat // tm, tiles_per_seq=seq // tm),
        out_shape=(out, out),
        grid=(rows // tm,),
        in_specs=[pl.BlockSpec((tm, 3 * CH), lambda i: (i, 0)),
                  pl.BlockSpec((8, 3 * CH), lambda i: (jnp.maximum(i * per8 - 1, 0), 0)),
                  pl.BlockSpec((8, 3 * CH), lambda i: (jnp.minimum((i + 1) * per8, last8), 0)),
                  pl.BlockSpec((None, 3, 3 * CH), lambda i: (l, 0, 0)),
                  pl.BlockSpec((None, 1, 3 * CH), lambda i: (l, 0, 0))],
        out_specs=(pl.BlockSpec((tm, CH), lambda i: (i, 0)), pl.BlockSpec((tm, CH), lambda i: (i, 0))),
        compiler_params=_cparams(1),
        name="hyena_short_conv",
    )(u, u, u, conv_w, conv_b)


RANK_TM = 512


def _expert_onehots(r):
    lane = lax.broadcasted_iota(jnp.int32, r.shape, 1).astype(jnp.float32)
    return (jnp.where(lane == r[:, 0:1], 1.0, 0.0), jnp.where(lane == r[:, 1:2], 1.0, 0.0))


def _lanes01(v0, v1, shape):
    lane = lax.broadcasted_iota(jnp.int32, shape, 1)
    return jnp.where(lane == 0, v0, jnp.where(lane == 1, v1, 0.0))


def _rank_kernel(r_ref, o_ref, c_ref, cnt_ref):
    @pl.when(pl.program_id(0) == 0)
    def _():
        cnt_ref[...] = jnp.zeros_like(cnt_ref)

    r = r_ref[...]
    tm = r.shape[0]
    oh0, oh1 = _expert_onehots(r)
    earlier = (lax.broadcasted_iota(jnp.int32, (tm, tm), 1)
               < lax.broadcasted_iota(jnp.int32, (tm, tm), 0))
    tri = jnp.where(earlier, 1.0, 0.0).astype(jnp.bfloat16)
    pre0 = jnp.dot(tri, oh0.astype(jnp.bfloat16), preferred_element_type=jnp.float32)
    pre1 = jnp.dot(tri, oh1.astype(jnp.bfloat16), preferred_element_type=jnp.float32)
    cnt = cnt_ref[...]
    tot0 = jnp.sum(oh0, 0, keepdims=True)
    rank0 = jnp.sum((pre0 + cnt) * oh0, -1, keepdims=True)
    rank1 = jnp.sum((pre1 + cnt + tot0) * oh1, -1, keepdims=True)
    o_ref[...] = _lanes01(rank0, rank1, r.shape)
    cnt_ref[...] = cnt + tot0 + jnp.sum(oh1, 0, keepdims=True)
    c_ref[...] = cnt_ref[...]


def _dest_kernel(r_ref, k_ref, ps_ref, o_ref):
    r = r_ref[...]
    k = k_ref[...]
    oh0, oh1 = _expert_onehots(r)
    ps = ps_ref[...]
    d0 = jnp.sum(oh0 * ps, -1, keepdims=True) + k[:, 0:1]
    d1 = jnp.sum(oh1 * ps, -1, keepdims=True) + k[:, 1:2]
    o_ref[...] = _lanes01(d0, d1, r.shape).astype(jnp.int32)


def _dispatch(route):
    tm = RANK_TM
    row = pl.BlockSpec((tm, ROUTE_LANES), lambda i: (i, 0))
    one = pl.BlockSpec((1, ROUTE_LANES), lambda i: (0, 0))
    rank, counts = pl.pallas_call(
        _rank_kernel,
        out_shape=(jax.ShapeDtypeStruct((NT, ROUTE_LANES), jnp.float32),
                   jax.ShapeDtypeStruct((1, ROUTE_LANES), jnp.float32)),
        grid=(NT // tm,),
        in_specs=[row],
        out_specs=(row, one),
        scratch_shapes=[pltpu.VMEM((1, ROUTE_LANES), jnp.float32)],
        compiler_params=_cparams(1),
        name="moe_rank",
    )(route)
    counts = counts[0].astype(jnp.int32)
    padded = ((counts + MOE_TB - 1) // MOE_TB) * MOE_TB
    pend = jnp.cumsum(padded)
    pstart = (pend - padded).astype(jnp.float32)[None, :]
    dest = pl.pallas_call(
        _dest_kernel,
        out_shape=jax.ShapeDtypeStruct((NT, ROUTE_LANES), jnp.int32),
        grid=(NT // tm,),
        in_specs=[row, row, one],
        out_specs=row,
        compiler_params=_cparams(1),
        name="moe_dest",
    )(route, rank, pstart)
    dest = dest[:, :TOP_K]
    out_row = (jnp.arange(TOP_K, dtype=jnp.int32)[None, :] * NT
               + jnp.arange(NT, dtype=jnp.int32)[:, None])
    slot_v = jnp.full((MOE_SLOTS,), -1, jnp.int32).at[dest.reshape(-1)].set(out_row.reshape(-1))
    is_pad = slot_v < 0
    slot_tok = jnp.where(is_pad, 0, slot_v % NT)
    slot_dst = jnp.where(is_pad, NT * TOP_K - 1 + jnp.cumsum(is_pad.astype(jnp.int32)), slot_v)
    block_e = jnp.minimum(jnp.searchsorted(pend[:N_EXPERTS],
                                           jnp.arange(MOE_BLOCKS, dtype=jnp.int32) * MOE_TB,
                                           side='right'), N_EXPERTS - 1).astype(jnp.int32)
    n_used = (pend[N_EXPERTS - 1] // MOE_TB).astype(jnp.int32).reshape(1)
    return (slot_tok.reshape(MOE_BLOCKS, 1, MOE_TB), slot_dst.reshape(MOE_BLOCKS, 1, MOE_TB),
            block_e, n_used)


def kernel(x, c, ctx, c_ctx, w_ada, b_ada, w_in, conv_w, conv_b, hy_w1, hy_b1, hy_w2, hy_b2, hy_w3, hy_b3, hy_w4, hy_b4, hy_freq, hy_dbias, attn_sink, w_br_attn, w_br_hyena, w_br_fnet, w_out, ln1_g, ln1_b, ln2_g, ln2_b, rg_w, rg_b, re_w, re_b, moe_w1, moe_w3, moe_w2):
    bf16 = jnp.bfloat16
    xs = jnp.concatenate([x.reshape(N_LAT, D_MODEL), ctx.reshape(N_CTX, D_MODEL)], 0)
    c8 = jnp.concatenate([c, c_ctx[None, :], jnp.zeros((MOD_ROWS - BATCH - 1, D_MODEL), c.dtype)], 0)
    mod4 = ada_call(c8, w_ada, b_ada).reshape(DEPTH, MOD_ROWS, 1, 6 * D_MODEL)
    cos_t, sin_t = rope_tables(512)
    ln1_g3, ln1_b3 = ln1_g.reshape(DEPTH, 1, D_MODEL), ln1_b.reshape(DEPTH, 1, D_MODEL)
    ln2_g3, ln2_b3 = ln2_g.reshape(DEPTH, 1, D_MODEL), ln2_b.reshape(DEPTH, 1, D_MODEL)
    pad = jnp.zeros((DEPTH, D_MODEL, ROUTE_LANES - N_GROUPS - N_EXPERTS), jnp.float32)
    w_route = jnp.concatenate([rg_w, re_w, pad], -1).astype(bf16)
    b_route = jnp.concatenate([rg_b, re_b, pad[:, 0]], -1).reshape(DEPTH, 1, ROUTE_LANES)

    hy_lat, hy_ctx = dft_consts(2 * SEQ, 64), dft_consts(2 * CTX_LEN, 16)
    fn_lat, fn_ctx = fnet_consts(SEQ, 64), fnet_consts(CTX_LEN, 16)
    tab_lat, tab_ctx = filter_tables(SEQ), filter_tables(CTX_LEN)
    conv_b3 = conv_b.reshape(DEPTH, 1, 3 * HYENA_WIDTH)

    h = pre_call(xs, mod4)
    for l in range(DEPTH):
        qkv = qkv_call(h, w_in, l, cos_t, sin_t)
        u_hy = proj_call(h, w_in, l, HY_OFF, 3 * HYENA_WIDTH, None, jnp.float32, "proj_hyena")
        u_fn = proj_call(h, w_in, l, FN_OFF, FNET_WIDTH, None, jnp.float32, "proj_fnet")
        gates = proj_call(h, w_in, l, GATE_OFF, 3 * D_MODEL, "sigmoid", bf16, "proj_gates")
        a = attn_call(qkv, attn_sink, l)
        filt = (hy_w1[l], hy_b1[l], hy_w2[l], hy_b2[l], hy_w3[l], hy_b3[l], hy_w4[l], hy_b4[l], hy_freq[l])
        z, x0 = sconv_call(u_hy, conv_w, conv_b3, l, N_LAT, SEQ)
        y_hy = jnp.concatenate(
            [hyena_long(z, x0, SEQ, BATCH, 0, hy_lat, tab_lat, filt, hy_dbias[l], "lat"),
             hyena_long(z, x0, CTX_LEN, BATCH, N_LAT, hy_ctx, tab_ctx, filt, hy_dbias[l], "ctx")], 0)
        y_fn = jnp.concatenate([fnet_seq(u_fn, SEQ, BATCH, 0, fn_lat, "lat"),
                                fnet_seq(u_fn, CTX_LEN, BATCH, N_LAT, fn_ctx, "ctx")], 0)
        xs, t = mix_call(a, y_hy, y_fn, gates, xs,
                         w_br_attn[l].astype(bf16), w_br_hyena[l].astype(bf16),
                         w_br_fnet[l].astype(bf16), w_out[l].astype(bf16),
                         mod4, ln1_g3, ln1_b3, l)
        route = route_call(t, w_route[l], b_route[l])
        slot_tok, slot_dst, block_e, n_used = _dispatch(route)
        yk = moe_call(block_e, n_used, slot_tok, slot_dst, t, moe_w1, moe_w3, moe_w2, l)
        xs, h = post_call(xs, yk, route, mod4, ln2_g3, ln2_b3, l)
    return xs[:N_LAT].reshape(BATCH, SEQ, D_MODEL)
```

```python
import functools
import math

import numpy as np
import jax
import jax.numpy as jnp
from jax import lax
from jax.experimental import pallas as pl
from jax.experimental.pallas import tpu as pltpu

D_MODEL = 2048
BATCH = 4
SEQ = 4096
DEPTH = 4
GRID_W = 64
CTX_LEN = 256

N_HEADS = 8
N_KV_HEADS = 2
HEAD_DIM = 128
Q_GROUP = N_HEADS // N_KV_HEADS
ATTN_WIDTH = N_HEADS * HEAD_DIM
KV_WIDTH = N_KV_HEADS * HEAD_DIM
WINDOW = 128
ROPE_THETA = 10000.0
MASK_VALUE = -1e30

HYENA_WIDTH = 512
HYENA_SHORT = 3
HYENA_BANDS = 16
HYENA_EMB = 1 + 2 * HYENA_BANDS
HYENA_FILTER_HIDDEN = 64
HYENA_TARGET = 1e-2
HYENA_MAX_DECAY = math.log(HYENA_TARGET) / 0.3
HYENA_MIN_DECAY = math.log(HYENA_TARGET) / 1.5

FNET_WIDTH = 512
FNET_GROUPS = 4
FNET_GROUP_DIM = FNET_WIDTH // FNET_GROUPS

Q_OFF = 0
K_OFF = Q_OFF + ATTN_WIDTH
V_OFF = K_OFF + KV_WIDTH
HY_OFF = V_OFF + KV_WIDTH
FN_OFF = HY_OFF + 3 * HYENA_WIDTH
GATE_OFF = FN_OFF + FNET_WIDTH
IN_WIDTH = GATE_OFF + 3 * D_MODEL
QKV_WIDTH = HY_OFF

N_GROUPS = 4
EXPERTS_PER_GROUP = 8
N_EXPERTS = N_GROUPS * EXPERTS_PER_GROUP
TOP_K = 2
EXPERT_HIDDEN = 512

LN_EPS = 1e-6
DEEPNORM_ALPHA = (2 * DEPTH) ** 0.25

N_LAT = BATCH * SEQ
N_CTX = BATCH * CTX_LEN
NT = N_LAT + N_CTX
MOD_ROWS = 8

LANE = 128
VMEM_LIMIT = 56 * 1024 * 1024

MOE_TB = 256
MOE_BLOCKS = -(-(NT * TOP_K + N_EXPERTS * (MOE_TB - 1)) // MOE_TB)
MOE_SLOTS = MOE_BLOCKS * MOE_TB


def _cparams(n_axes):
    return pltpu.CompilerParams(dimension_semantics=("arbitrary",) * n_axes,
                                vmem_limit_bytes=VMEM_LIMIT)


def _mod_row(i, tm):
    return jnp.minimum((i * tm) // SEQ, BATCH)


def _mod_spec(l, chunk, tm, axis=0):
    def imap(*g):
        return (l, _mod_row(g[axis], tm), 0, chunk)
    return pl.BlockSpec((None, None, 1, D_MODEL), imap)


def _ln(x):
    mu = jnp.mean(x, -1, keepdims=True)
    xc = x - mu
    var = jnp.mean(xc * xc, -1, keepdims=True)
    return xc * lax.rsqrt(var + LN_EPS)


def _ada_kernel(c_ref, w_ref, b_ref, o_ref):
    c = c_ref[...]
    a = (c * jax.nn.sigmoid(c)).astype(jnp.bfloat16)
    o_ref[...] = jnp.dot(a, w_ref[...].astype(jnp.bfloat16),
                         preferred_element_type=jnp.float32) + b_ref[...]


def ada_call(c8, w_ada, b_ada):
    tn = 1024
    n = 6 * D_MODEL
    return pl.pallas_call(
        _ada_kernel,
        out_shape=jax.ShapeDtypeStruct((DEPTH, MOD_ROWS, n), jnp.float32),
        grid=(DEPTH, n // tn),
        in_specs=[pl.BlockSpec((MOD_ROWS, D_MODEL), lambda l, j: (0, 0)),
                  pl.BlockSpec((None, D_MODEL, tn), lambda l, j: (l, 0, j)),
                  pl.BlockSpec((None, 1, tn), lambda l, j: (l, 0, j))],
        out_specs=pl.BlockSpec((None, MOD_ROWS, tn), lambda l, j: (l, 0, j)),
        compiler_params=_cparams(2),
        name="adaln",
    )(c8, w_ada, b_ada.reshape(DEPTH, 1, n))


def _pre_kernel(x_ref, sh_ref, sc_ref, h_ref):
    h_ref[...] = (_ln(x_ref[...]) * (1.0 + sc_ref[...]) + sh_ref[...]).astype(h_ref.dtype)


def pre_call(x, mod4):
    tm = 512
    return pl.pallas_call(
        _pre_kernel,
        out_shape=jax.ShapeDtypeStruct((NT, D_MODEL), jnp.bfloat16),
        grid=(NT // tm,),
        in_specs=[pl.BlockSpec((tm, D_MODEL), lambda i: (i, 0)),
                  _mod_spec(0, 0, tm), _mod_spec(0, 1, tm)],
        out_specs=pl.BlockSpec((tm, D_MODEL), lambda i: (i, 0)),
        compiler_params=_cparams(1),
        name="pre_modulate",
    )(x, mod4, mod4)


def _proj_kernel(x_ref, w_ref, o_ref, wb_ref, *, act):
    @pl.when(pl.program_id(1) == 0)
    def _():
        wb_ref[...] = w_ref[...].astype(jnp.bfloat16)

    acc = jnp.dot(x_ref[...], wb_ref[...], preferred_element_type=jnp.float32)
    if act == "sigmoid":
        acc = jax.nn.sigmoid(acc)
    o_ref[...] = acc.astype(o_ref.dtype)


def proj_call(h, w_in, l, col_off, width, act, out_dtype, name):
    tm, tn = 1024, 512
    off = col_off // tn
    return pl.pallas_call(
        functools.partial(_proj_kernel, act=act),
        out_shape=jax.ShapeDtypeStruct((NT, width), out_dtype),
        grid=(width // tn, NT // tm),
        in_specs=[pl.BlockSpec((tm, D_MODEL), lambda j, i: (i, 0)),
                  pl.BlockSpec((None, D_MODEL, tn), lambda j, i: (l, 0, off + j))],
        out_specs=pl.BlockSpec((tm, tn), lambda j, i: (i, j)),
        scratch_shapes=[pltpu.VMEM((D_MODEL, tn), jnp.bfloat16)],
        compiler_params=_cparams(2),
        name=name,
    )(h, w_in)


def _swap32(x):
    lane = lax.broadcasted_iota(jnp.int32, x.shape, 1)
    up = pltpu.roll(x, LANE - 32, 1)
    dn = pltpu.roll(x, 32, 1)
    return jnp.where((lane & 63) < 32, up, dn)


def _qkv_kernel(x_ref, w_ref, cos_ref, sin_ref, o_ref, wb_ref):
    @pl.when(pl.program_id(0) == 0)
    def _():
        wb_ref[...] = w_ref[...].astype(jnp.bfloat16)

    cos = cos_ref[...]
    sin = sin_ref[...]
    scale = HEAD_DIM ** -0.5
    for hd in range(QKV_WIDTH // HEAD_DIM):
        sl = slice(hd * HEAD_DIM, (hd + 1) * HEAD_DIM)
        acc = jnp.dot(x_ref[...], wb_ref[:, sl], preferred_element_type=jnp.float32)
        if hd < N_HEADS + N_KV_HEADS:
            acc = acc * cos + _swap32(acc) * sin
        if hd < N_HEADS:
            acc = acc * scale
        o_ref[:, sl] = acc.astype(o_ref.dtype)


def qkv_call(h, w_in, l, cos_t, sin_t):
    tm = 512
    n_pos = SEQ // tm

    def rope_map(i):
        return (jnp.where(i * tm < N_LAT, i % n_pos, n_pos), 0)

    return pl.pallas_call(
        _qkv_kernel,
        out_shape=jax.ShapeDtypeStruct((NT, QKV_WIDTH), jnp.bfloat16),
        grid=(NT // tm,),
        in_specs=[pl.BlockSpec((tm, D_MODEL), lambda i: (i, 0)),
                  pl.BlockSpec((None, D_MODEL, QKV_WIDTH), lambda i: (l, 0, 0),
                               pipeline_mode=pl.Buffered(1)),
                  pl.BlockSpec((tm, HEAD_DIM), rope_map),
                  pl.BlockSpec((tm, HEAD_DIM), rope_map)],
        out_specs=pl.BlockSpec((tm, QKV_WIDTH), lambda i: (i, 0)),
        scratch_shapes=[pltpu.VMEM((D_MODEL, QKV_WIDTH), jnp.bfloat16)],
        compiler_params=_cparams(1),
        name="qkv_rope",
    )(h, w_in, cos_t, sin_t)


def rope_tables(tm):
    n_freq = HEAD_DIM // 4
    inv_freq = ROPE_THETA ** (-np.arange(n_freq, dtype=np.float64) / n_freq)
    t = np.arange(SEQ)
    ang_row = (t // GRID_W)[:, None] * inv_freq[None, :]
    ang_col = (t % GRID_W)[:, None] * inv_freq[None, :]
    cos = np.concatenate([np.cos(ang_row), np.cos(ang_row), np.cos(ang_col), np.cos(ang_col)], 1)
    sin = np.concatenate([-np.sin(ang_row), np.sin(ang_row), -np.sin(ang_col), np.sin(ang_col)], 1)
    cos = np.concatenate([cos, np.ones((tm, HEAD_DIM))], 0)
    sin = np.concatenate([sin, np.zeros((tm, HEAD_DIM))], 0)
    return jnp.asarray(cos, jnp.float32), jnp.asarray(sin, jnp.float32)


ATT_QT = 256
ATT_LAT_TILES = N_LAT // ATT_QT
ATT_TILES_PER_SEQ = SEQ // ATT_QT


def _attn_kernel(sink_ref, q_ref, kp_ref, kc_ref, kn_ref, vp_ref, vc_ref, vn_ref,
                 kx_ref, vx_ref, o_ref, *, l):
    i = pl.program_id(0)
    h = pl.program_id(1)
    is_lat = i < ATT_LAT_TILES
    base = (i % ATT_TILES_PER_SEQ) * ATT_QT
    n_loc = ATT_QT + 2 * WINDOW
    r = lax.broadcasted_iota(jnp.int32, (ATT_QT, n_loc), 0)
    j = lax.broadcasted_iota(jnp.int32, (ATT_QT, n_loc), 1)
    kpos = base - WINDOW + j
    dist = j - r
    k_end = jnp.where(is_lat, SEQ, 0)
    valid = (dist >= 0) & (dist <= 2 * WINDOW) & (kpos >= 0) & (kpos < k_end)

    k_loc = jnp.concatenate([kp_ref[...], kc_ref[...], kn_ref[...]], axis=0)
    v_loc = jnp.concatenate([vp_ref[...], vc_ref[...], vn_ref[...]], axis=0)
    k_ctx = kx_ref[...]
    v_ctx = vx_ref[...]
    dn = (((1,), (1,)), ((), ()))
    for g in range(Q_GROUP):
        sl = slice(g * HEAD_DIM, (g + 1) * HEAD_DIM)
        q = q_ref[:, sl]
        s_loc = lax.dot_general(q, k_loc, dn, preferred_element_type=jnp.float32)
        s_loc = jnp.where(valid, s_loc, MASK_VALUE)
        s_ctx = lax.dot_general(q, k_ctx, dn, preferred_element_type=jnp.float32)
        sink = sink_ref[l, h * Q_GROUP + g]
        m = jnp.maximum(jnp.maximum(jnp.max(s_loc, -1, keepdims=True),
                                    jnp.max(s_ctx, -1, keepdims=True)), sink)
        e_loc = jnp.exp(s_loc - m)
        e_ctx = jnp.exp(s_ctx - m)
        den = (jnp.sum(e_loc, -1, keepdims=True) + jnp.sum(e_ctx, -1, keepdims=True)
               + jnp.exp(sink - m))
        o = (jnp.dot(e_loc.astype(jnp.bfloat16), v_loc, preferred_element_type=jnp.float32)
             + jnp.dot(e_ctx.astype(jnp.bfloat16), v_ctx, preferred_element_type=jnp.float32))
        o_ref[:, sl] = (o / den).astype(o_ref.dtype)


def attn_call(qkv, sink, l):
    blk = WINDOW
    per = ATT_QT // blk
    last_blk = NT // blk - 1
    kcol, vcol = K_OFF // HEAD_DIM, V_OFF // HEAD_DIM

    def batch_of(i):
        return jnp.where(i < ATT_LAT_TILES, i // ATT_TILES_PER_SEQ, i - ATT_LAT_TILES)

    def prev_map(col):
        return lambda i, h: (jnp.maximum(i * per - 1, 0), col + h)

    def next_map(col):
        return lambda i, h: (jnp.minimum(i * per + per, last_blk), col + h)

    def cur_map(col):
        return lambda i, h: (i, col + h)

    def ctx_map(col):
        return lambda i, h: (N_LAT // CTX_LEN + batch_of(i), col + h)

    return pl.pallas_call(
        functools.partial(_attn_kernel, l=l),
        out_shape=jax.ShapeDtypeStruct((NT, ATTN_WIDTH), jnp.bfloat16),
        grid=(NT // ATT_QT, N_KV_HEADS),
        in_specs=[pl.BlockSpec(memory_space=pltpu.SMEM),
                  pl.BlockSpec((ATT_QT, Q_GROUP * HEAD_DIM), lambda i, h: (i, h)),
                  pl.BlockSpec((blk, HEAD_DIM), prev_map(kcol)),
                  pl.BlockSpec((ATT_QT, HEAD_DIM), cur_map(kcol)),
                  pl.BlockSpec((blk, HEAD_DIM), next_map(kcol)),
                  pl.BlockSpec((blk, HEAD_DIM), prev_map(vcol)),
                  pl.BlockSpec((ATT_QT, HEAD_DIM), cur_map(vcol)),
                  pl.BlockSpec((blk, HEAD_DIM), next_map(vcol)),
                  pl.BlockSpec((CTX_LEN, HEAD_DIM), ctx_map(kcol)),
                  pl.BlockSpec((CTX_LEN, HEAD_DIM), ctx_map(vcol))],
        out_specs=pl.BlockSpec((ATT_QT, Q_GROUP * HEAD_DIM), lambda i, h: (i, h)),
        compiler_params=_cparams(2),
        name="window_attention",
    )(sink, qkv, qkv, qkv, qkv, qkv, qkv, qkv, qkv, qkv)


def _mix_kernel(a_ref, yh_ref, yf_ref, g_ref, x_ref, wa_ref, wh_ref, wf_ref, wo_ref,
                g1_ref, sh2_ref, sc2_ref, lg_ref, lb_ref, xo_ref, t_ref, y_ref):
    tn = 512
    for c in range(D_MODEL // tn):
        sl = slice(c * tn, (c + 1) * tn)
        ya = jnp.dot(a_ref[...], wa_ref[:, sl], preferred_element_type=jnp.float32)
        yh = jnp.dot(yh_ref[...], wh_ref[:, sl], preferred_element_type=jnp.float32)
        yf = jnp.dot(yf_ref[...], wf_ref[:, sl], preferred_element_type=jnp.float32)
        y = (g_ref[:, sl].astype(jnp.float32) * ya
             + g_ref[:, D_MODEL + c * tn:D_MODEL + (c + 1) * tn].astype(jnp.float32) * yh
             + g_ref[:, 2 * D_MODEL + c * tn:2 * D_MODEL + (c + 1) * tn].astype(jnp.float32) * yf)
        y_ref[:, sl] = y.astype(y_ref.dtype)
    o = jnp.dot(y_ref[...], wo_ref[...], preferred_element_type=jnp.float32)
    xn = _ln(DEEPNORM_ALPHA * x_ref[...] + g1_ref[...] * o) * lg_ref[...] + lb_ref[...]
    xo_ref[...] = xn
    t_ref[...] = (_ln(xn) * (1.0 + sc2_ref[...]) + sh2_ref[...]).astype(t_ref.dtype)


def mix_call(a, yh, yf, gates, x, wa, wh, wf, wo, mod4, ln_g, ln_b, l):
    tm = 256
    row = lambda i: (i, 0)
    full = lambda i: (0, 0)
    const = pl.Buffered(1)
    return pl.pallas_call(
        _mix_kernel,
        out_shape=(jax.ShapeDtypeStruct((NT, D_MODEL), jnp.float32),
                   jax.ShapeDtypeStruct((NT, D_MODEL), jnp.float32)),
        grid=(NT // tm,),
        in_specs=[pl.BlockSpec((tm, ATTN_WIDTH), row),
                  pl.BlockSpec((tm, HYENA_WIDTH), row),
                  pl.BlockSpec((tm, FNET_WIDTH), row),
                  pl.BlockSpec((tm, 3 * D_MODEL), row),
                  pl.BlockSpec((tm, D_MODEL), row),
                  pl.BlockSpec((ATTN_WIDTH, D_MODEL), full, pipeline_mode=const),
                  pl.BlockSpec((HYENA_WIDTH, D_MODEL), full, pipeline_mode=const),
                  pl.BlockSpec((FNET_WIDTH, D_MODEL), full, pipeline_mode=const),
                  pl.BlockSpec((D_MODEL, D_MODEL), full, pipeline_mode=const),
                  _mod_spec(l, 2, tm), _mod_spec(l, 3, tm), _mod_spec(l, 4, tm),
                  pl.BlockSpec((None, 1, D_MODEL), lambda i: (l, 0, 0)),
                  pl.BlockSpec((None, 1, D_MODEL), lambda i: (l, 0, 0))],
        out_specs=(pl.BlockSpec((tm, D_MODEL), row), pl.BlockSpec((tm, D_MODEL), row)),
        scratch_shapes=[pltpu.VMEM((tm, D_MODEL), jnp.bfloat16)],
        compiler_params=_cparams(1),
        name="mixer_out",
    )(a, yh, yf, gates, x, wa, wh, wf, wo, mod4, mod4, mod4, ln_g, ln_b)


ROUTE_LANES = LANE


def _route_kernel(t_ref, w_ref, b_ref, o_ref):
    logits = jnp.dot(t_ref[...].astype(jnp.bfloat16), w_ref[...],
                     preferred_element_type=jnp.float32) + b_ref[...]
    lane = lax.broadcasted_iota(jnp.int32, logits.shape, 1)
    neg = -jnp.inf
    big = jnp.int32(2 ** 30)

    def first_argmax(vals):
        mx = jnp.max(vals, -1, keepdims=True)
        idx = jnp.min(jnp.where(vals == mx, lane, big), -1, keepdims=True)
        return mx, idx

    g_vals = jnp.where(lane < N_GROUPS, logits, neg)
    g_max, grp = first_argmax(g_vals)
    g_w = 1.0 / jnp.sum(jnp.exp(g_vals - g_max), -1, keepdims=True)
    lo = N_GROUPS + grp * EXPERTS_PER_GROUP
    e_vals = jnp.where((lane >= lo) & (lane < lo + EXPERTS_PER_GROUP), logits, neg)
    v1, i1 = first_argmax(e_vals)
    v2, i2 = first_argmax(jnp.where(lane == i1, neg, e_vals))
    e2 = jnp.exp(v2 - v1)
    w1 = g_w / (1.0 + e2)
    w2 = g_w * e2 / (1.0 + e2)
    id1 = (i1 - N_GROUPS).astype(jnp.float32)
    id2 = (i2 - N_GROUPS).astype(jnp.float32)
    o_ref[...] = jnp.where(lane == 0, id1, jnp.where(lane == 1, id2,
                           jnp.where(lane == 2, w1, jnp.where(lane == 3, w2, 0.0))))


def route_call(t, wr, br):
    tm = 512
    return pl.pallas_call(
        _route_kernel,
        out_shape=jax.ShapeDtypeStruct((NT, ROUTE_LANES), jnp.float32),
        grid=(NT // tm,),
        in_specs=[pl.BlockSpec((tm, D_MODEL), lambda i: (i, 0)),
                  pl.BlockSpec((D_MODEL, ROUTE_LANES), lambda i: (0, 0)),
                  pl.BlockSpec((1, ROUTE_LANES), lambda i: (0, 0))],
        out_specs=pl.BlockSpec((tm, ROUTE_LANES), lambda i: (i, 0)),
        compiler_params=_cparams(1),
        name="moe_router",
    )(t, wr, br)


MOE_OUT_ROWS = MOE_SLOTS


def _moe_kernel(be_ref, nu_ref, gi_ref, gn_ref, si_ref, t_hbm, w1_ref, w3_ref, w2_ref, o_hbm,
                b1_ref, b3_ref, b2_ref, xbuf, obuf, gsem, ssem):
    i = pl.program_id(0)

    def gather(idx_ref, s):
        for r in range(MOE_TB):
            pltpu.make_async_copy(t_hbm.at[pl.ds(idx_ref[0, r], 1)], xbuf.at[s, pl.ds(r, 1)],
                                  gsem.at[s]).start()

    def wait_rows(src, dst, sem):
        pltpu.make_async_copy(src, dst, sem).wait()

    @pl.when(i == 0)
    def _():
        gather(gi_ref, 0)

    first = (i == 0) | (be_ref[i] != be_ref[jnp.maximum(i - 1, 0)])

    def step(slot):
        wait_rows(t_hbm.at[pl.ds(0, MOE_TB)], xbuf.at[slot], gsem.at[slot])

        @pl.when(i + 1 < MOE_BLOCKS)
        def _():
            gather(gn_ref, 1 - slot)

        @pl.when(first)
        def _():
            b1_ref[...] = w1_ref[...].astype(jnp.bfloat16)
            b3_ref[...] = w3_ref[...].astype(jnp.bfloat16)
            b2_ref[...] = w2_ref[...].astype(jnp.bfloat16)

        @pl.when(i >= 2)
        def _():
            wait_rows(obuf.at[slot], o_hbm.at[pl.ds(0, MOE_TB)], ssem.at[slot])

        @pl.when(i < nu_ref[0])
        def _():
            x = xbuf[slot].astype(jnp.bfloat16)
            h1 = jnp.dot(x, b1_ref[...], preferred_element_type=jnp.float32)
            h3 = jnp.dot(x, b3_ref[...], preferred_element_type=jnp.float32)
            hh = (h1 * jax.nn.sigmoid(h1) * h3).astype(jnp.bfloat16)
            obuf[slot] = jnp.dot(hh, b2_ref[...], preferred_element_type=jnp.float32)

        @pl.when(i >= nu_ref[0])
        def _():
            obuf[slot] = jnp.zeros(obuf.shape[1:], obuf.dtype)

        for r in range(MOE_TB):
            pltpu.make_async_copy(obuf.at[slot, pl.ds(r, 1)], o_hbm.at[pl.ds(si_ref[0, r], 1)],
                                  ssem.at[slot]).start()

        @pl.when(i == MOE_BLOCKS - 1)
        def _():
            wait_rows(obuf.at[1 - slot], o_hbm.at[pl.ds(0, MOE_TB)], ssem.at[1 - slot])
            wait_rows(obuf.at[slot], o_hbm.at[pl.ds(0, MOE_TB)], ssem.at[slot])

    for parity in range(2):
        pl.when(i % 2 == parity)(functools.partial(step, parity))


def moe_call(block_e, n_used, slot_tok, slot_dst, t, w1, w3, w2, l):
    smem_blk = lambda f: pl.BlockSpec((None, 1, MOE_TB), f, memory_space=pltpu.SMEM)
    grid_spec = pltpu.PrefetchScalarGridSpec(
        num_scalar_prefetch=2,
        grid=(MOE_BLOCKS,),
        in_specs=[smem_blk(lambda i, be, nu: (i, 0, 0)),
                  smem_blk(lambda i, be, nu: (jnp.minimum(i + 1, MOE_BLOCKS - 1), 0, 0)),
                  smem_blk(lambda i, be, nu: (i, 0, 0)),
                  pl.BlockSpec(memory_space=pl.ANY),
                  pl.BlockSpec((None, None, D_MODEL, EXPERT_HIDDEN), lambda i, be, nu: (l, be[i], 0, 0)),
                  pl.BlockSpec((None, None, D_MODEL, EXPERT_HIDDEN), lambda i, be, nu: (l, be[i], 0, 0)),
                  pl.BlockSpec((None, None, EXPERT_HIDDEN, D_MODEL), lambda i, be, nu: (l, be[i], 0, 0))],
        out_specs=pl.BlockSpec(memory_space=pl.ANY),
        scratch_shapes=[pltpu.VMEM((D_MODEL, EXPERT_HIDDEN), jnp.bfloat16),
                        pltpu.VMEM((D_MODEL, EXPERT_HIDDEN), jnp.bfloat16),
                        pltpu.VMEM((EXPERT_HIDDEN, D_MODEL), jnp.bfloat16),
                        pltpu.VMEM((2, MOE_TB, D_MODEL), jnp.float32),
                        pltpu.VMEM((2, MOE_TB, D_MODEL), jnp.float32),
                        pltpu.SemaphoreType.DMA((2,)),
                        pltpu.SemaphoreType.DMA((2,))])
    return pl.pallas_call(
        _moe_kernel,
        out_shape=jax.ShapeDtypeStruct((MOE_OUT_ROWS, D_MODEL), jnp.float32),
        grid_spec=grid_spec,
        compiler_params=pltpu.CompilerParams(dimension_semantics=("arbitrary",),
                                             vmem_limit_bytes=VMEM_LIMIT, has_side_effects=True),
        name="moe_experts",
    )(block_e, n_used, slot_tok, slot_tok, slot_dst, t, w1, w3, w2)


def _post_kernel(x_ref, y0_ref, y1_ref, r_ref, g2_ref, lg_ref, lb_ref, *rest):
    r = r_ref[...]
    m = r[:, 2:3] * y0_ref[...] + r[:, 3:4] * y1_ref[...]
    xn = _ln(DEEPNORM_ALPHA * x_ref[...] + g2_ref[...] * m) * lg_ref[...] + lb_ref[...]
    if len(rest) == 1:
        rest[0][...] = xn
    else:
        sh_ref, sc_ref, xo_ref, h_ref = rest
        xo_ref[...] = xn
        h_ref[...] = (_ln(xn) * (1.0 + sc_ref[...]) + sh_ref[...]).astype(h_ref.dtype)


def post_call(x, yk, route, mod4, ln_g, ln_b, l):
    tm = 512
    last = l == DEPTH - 1
    rows = N_LAT if last else NT
    row = lambda i: (i, 0)
    x_out = jax.ShapeDtypeStruct((rows, D_MODEL), jnp.float32)
    x_spec = pl.BlockSpec((tm, D_MODEL), row)
    in_specs = [x_spec, x_spec,
                pl.BlockSpec((tm, D_MODEL), lambda i: (NT // tm + i, 0)),
                pl.BlockSpec((tm, ROUTE_LANES), row),
                _mod_spec(l, 5, tm),
                pl.BlockSpec((None, 1, D_MODEL), lambda i: (l, 0, 0)),
                pl.BlockSpec((None, 1, D_MODEL), lambda i: (l, 0, 0))]
    args = [x, yk, yk, route, mod4, ln_g, ln_b]
    if last:
        out_shape, out_specs = x_out, x_spec
    else:
        in_specs += [_mod_spec(l + 1, 0, tm), _mod_spec(l + 1, 1, tm)]
        args += [mod4, mod4]
        out_shape = (x_out, jax.ShapeDtypeStruct((rows, D_MODEL), jnp.bfloat16))
        out_specs = (x_spec, x_spec)
    return pl.pallas_call(
        _post_kernel,
        out_shape=out_shape,
        grid=(rows // tm,),
        in_specs=in_specs,
        out_specs=out_specs,
        compiler_params=_cparams(1),
        name="moe_combine_postln",
    )(*args)


CH = HYENA_WIDTH


def _hi_lo(x):
    hi = x.astype(jnp.bfloat16)
    lo = (x - hi.astype(jnp.float32)).astype(jnp.bfloat16)
    return hi, lo


def _const_lhs3(m):
    hi, lo = _hi_lo(jnp.asarray(m, jnp.float32))
    return jnp.concatenate([hi, lo, hi], 1)


def _data_rhs3(x):
    hi, lo = _hi_lo(x)
    return jnp.concatenate([hi, hi, lo], 0)


def _const_rhs3(m):
    hi, lo = _hi_lo(jnp.asarray(m, jnp.float32))
    return jnp.concatenate([hi, hi, lo], 0)


def _data_lhs3(x):
    hi, lo = _hi_lo(x)
    return jnp.concatenate([hi, lo, hi], 1)


def _cs(n, rows, cols):
    ang = 2.0 * np.pi * np.outer(np.arange(rows), np.arange(cols)) / n
    return np.cos(ang), np.sin(ang)


def dft_consts(n_fft, n1):
    n2 = n_fft // n1
    c1, s1 = _cs(n1, n1, n1)
    c2, s2 = _cs(n2, n2, n2)
    h = n1 // 2
    tr, ti = _cs(n_fft, n1, n2)
    return dict(
        n1=n1, n2=n2,
        lead_real=_const_lhs3(np.concatenate([c1, -s1], 0)),
        lead_pair=_const_lhs3(np.block([[c1[:, :h], s1[:, :h]], [-s1[:, :h], c1[:, :h]]])),
        lead_inv=_const_lhs3(np.block([[c1[:h, :], -s1[:h, :]], [s1[:h, :], c1[:h, :]]]) / n_fft),
        f2=_const_lhs3(np.block([[c2, s2], [-s2, c2]])),
        f2_inv=_const_lhs3(np.block([[c2, -s2], [s2, c2]])),
        tw_r=jnp.asarray(tr[:, :, None], jnp.float32),
        tw_i=jnp.asarray(-ti[:, :, None], jnp.float32),
    )


def _lead_kernel(m_ref, x_ref, o_ref):
    o_ref[...] = jnp.dot(m_ref[...], _data_rhs3(x_ref[...]), preferred_element_type=jnp.float32)


def lead_call(mcat, x2d, k, row_off, groups, name):
    mo = mcat.shape[0]
    width = x2d.shape[1]
    tn = min(width, 8192)
    off = row_off // k
    return pl.pallas_call(
        _lead_kernel,
        out_shape=jax.ShapeDtypeStruct((groups, mo, width), jnp.float32),
        grid=(groups, width // tn),
        in_specs=[pl.BlockSpec(mcat.shape, lambda g, j: (0, 0)),
                  pl.BlockSpec((k, tn), lambda g, j: (off + g, j))],
        out_specs=pl.BlockSpec((None, mo, tn), lambda g, j: (g, 0, j)),
        compiler_params=_cparams(2),
        name=name,
    )(mcat, x2d)


def _lead_out_kernel(m_ref, d_ref, z_ref, x0_ref, db_ref, s_ref, o_ref):
    acc = jnp.dot(m_ref[...], _data_rhs3(d_ref[...]), preferred_element_type=jnp.float32)
    y = acc * (1.0 / (s_ref[...] + 1e-6)) + z_ref[...] * db_ref[...]
    o_ref[...] = (x0_ref[...] * y).astype(o_ref.dtype)


def lead_out_call(mcat, d3, z2d, x02d, dbias_t, ssum_t, k_out, row_off, name):
    groups, k_in, width = d3.shape
    tn = min(width, 8192)
    off = row_off // k_out
    view = pl.BlockSpec((k_out, tn), lambda g, j: (off + g, j))
    lanes = pl.BlockSpec((1, tn), lambda g, j: (0, 0))
    return pl.pallas_call(
        _lead_out_kernel,
        out_shape=jax.ShapeDtypeStruct((groups, k_out, width), jnp.bfloat16),
        grid=(groups, width // tn),
        in_specs=[pl.BlockSpec(mcat.shape, lambda g, j: (0, 0)),
                  pl.BlockSpec((None, k_in, tn), lambda g, j: (g, 0, j)),
                  view, view, lanes, lanes],
        out_specs=pl.BlockSpec((None, k_out, tn), lambda g, j: (g, 0, j)),
        compiler_params=_cparams(2),
        name=name,
    )(mcat, d3, z2d, x02d, dbias_t, ssum_t)


def _hy_slab_kernel(*refs, kb, n2, conv):
    if conv:
        ar_ref, ai_ref, tr_ref, ti_ref, f_ref, fi_ref, hr_ref, hi_ref, o_ref = refs
    else:
        ar_ref, ai_ref, tr_ref, ti_ref, f_ref, o_ref = refs
    f = f_ref[...]
    for j in range(kb):
        ar, ai, tr, ti = ar_ref[j], ai_ref[j], tr_ref[j], ti_ref[j]
        br = ar * tr - ai * ti
        bi = ar * ti + ai * tr
        x = jnp.dot(f, _data_rhs3(jnp.concatenate([br, bi], 0)), preferred_element_type=jnp.float32)
        xr, xi = x[:n2], x[n2:]
        if conv:
            hr, hi = hr_ref[j], hi_ref[j]
            yr = xr * hr - xi * hi
            yi = xr * hi + xi * hr
            d = jnp.dot(fi_ref[...], _data_rhs3(jnp.concatenate([yr, yi], 0)),
                        preferred_element_type=jnp.float32)
            dr, di = d[:n2], d[n2:]
            o_ref[0, j] = dr * tr + di * ti
            o_ref[1, j] = di * tr - dr * ti
        else:
            o_ref[0, j] = xr
            o_ref[1, j] = xi


def hy_slab_call(a3, cst, h5, name):
    n1, n2 = cst["n1"], cst["n2"]
    groups = a3.shape[0]
    kb = 8
    a5 = a3.reshape(groups, 2, n1, n2, CH)
    slab = lambda part: pl.BlockSpec((None, None, kb, n2, CH), lambda b, g: (g, part, b, 0, 0))
    tw = pl.BlockSpec((kb, n2, 1), lambda b, g: (b, 0, 0))
    mat = pl.BlockSpec(cst["f2"].shape, lambda b, g: (0, 0))
    in_specs = [slab(0), slab(1), tw, tw, mat]
    args = [a5, a5, cst["tw_r"], cst["tw_i"], cst["f2"]]
    conv = h5 is not None
    if conv:
        hs = lambda part: pl.BlockSpec((None, None, kb, n2, CH), lambda b, g: (0, part, b, 0, 0))
        in_specs += [mat, hs(0), hs(1)]
        args += [cst["f2_inv"], h5, h5]
    return pl.pallas_call(
        functools.partial(_hy_slab_kernel, kb=kb, n2=n2, conv=conv),
        out_shape=jax.ShapeDtypeStruct((groups, 2, n1, n2, CH), jnp.float32),
        grid=(n1 // kb, groups),
        in_specs=in_specs,
        out_specs=pl.BlockSpec((None, 2, kb, n2, CH), lambda b, g: (g, 0, b, 0, 0)),
        compiler_params=_cparams(2),
        name=name,
    )(*args)


def filter_tables(n):
    m = np.arange(2 * n)
    pos = np.where(m < n, m, 2 * n - m)
    pos = np.where(m == n, 0, pos)
    t = pos / (n - 1.0)
    w = 2.0 * np.pi * pos / n
    bands = np.linspace(1e-4, HYENA_BANDS - 1, HYENA_BANDS)
    feat = np.zeros((2 * n, LANE))
    feat[:, 0] = t
    feat[:, 1:1 + HYENA_BANDS] = np.cos(bands[None, :] * w[:, None])
    feat[:, 1 + HYENA_BANDS:HYENA_EMB] = -np.sin(bands[None, :] * w[:, None])
    aux = np.zeros((2 * n, LANE))
    aux[:, 0] = t
    aux[:, 1] = (m < n)
    aux[:, 2] = (m > n)
    deltas = np.abs(np.linspace(HYENA_MIN_DECAY, HYENA_MAX_DECAY, HYENA_WIDTH))[None, :]
    return (jnp.asarray(feat, jnp.float32), jnp.asarray(aux, jnp.float32),
            jnp.asarray(deltas, jnp.float32))


def _filt_kernel(z_ref, aux_ref, dl_ref, w1_ref, b1_ref, w2_ref, b2_ref, w3_ref, b3_ref,
                 w4_ref, b4_ref, fr_ref, o_ref, s_ref):
    hp = lax.Precision.HIGHEST
    f32 = jnp.float32
    fr = fr_ref[...]
    h = jnp.sin(fr[0:1] * (jnp.dot(z_ref[...], w1_ref[...], precision=hp, preferred_element_type=f32)
                           + b1_ref[...]))
    h = jnp.sin(fr[1:2] * (jnp.dot(h, w2_ref[...], precision=hp, preferred_element_type=f32)
                           + b2_ref[...]))
    h = jnp.sin(fr[2:3] * (jnp.dot(h, w3_ref[...], precision=hp, preferred_element_type=f32)
                           + b3_ref[...]))
    f = jnp.dot(h, w4_ref[...], precision=hp, preferred_element_type=f32) + b4_ref[...]
    aux = aux_ref[...]
    dec = jnp.exp(-aux[:, 0:1] * dl_ref[...])
    c = (f[:, :HYENA_WIDTH] * aux[:, 1:2] + f[:, HYENA_WIDTH:] * aux[:, 2:3]) * dec
    o_ref[...] = c

    @pl.when(pl.program_id(0) == 0)
    def _():
        s_ref[...] = jnp.zeros_like(s_ref)

    s_ref[...] += jnp.sum(jnp.abs(c), 0, keepdims=True)


def filter_call(n, tabs, w1p, b1, w2, b2, w3, b3, w4, b4, freq):
    feat, aux, deltas = tabs
    tr = min(2 * n, 1024)
    row = lambda i: (i, 0)
    full = lambda a: pl.BlockSpec(a.shape, lambda i: (0,) * a.ndim)
    ws = [w1p, b1, w2, b2, w3, b3, w4, b4, freq]
    return pl.pallas_call(
        _filt_kernel,
        out_shape=(jax.ShapeDtypeStruct((2 * n, HYENA_WIDTH), jnp.float32),
                   jax.ShapeDtypeStruct((1, HYENA_WIDTH), jnp.float32)),
        grid=(2 * n // tr,),
        in_specs=[pl.BlockSpec((tr, LANE), row), pl.BlockSpec((tr, LANE), row), full(deltas)]
                 + [full(a) for a in ws],
        out_specs=(pl.BlockSpec((tr, HYENA_WIDTH), row),
                   pl.BlockSpec((1, HYENA_WIDTH), lambda i: (0, 0))),
        compiler_params=_cparams(1),
        name="hyena_filter",
    )(feat, aux, deltas, *ws)


def fnet_consts(n, n1):
    cst = dft_consts(n, n1)
    c, s = _cs(FNET_GROUP_DIM, FNET_GROUP_DIM, FNET_GROUP_DIM)
    cst["chan"] = _const_rhs3(np.concatenate([c, s], 0) / math.sqrt(n * FNET_GROUP_DIM))
    return cst


def _fn_slab_kernel(ar_ref, ai_ref, tr_ref, ti_ref, f_ref, cs_ref, o_ref, *, kb, n2):
    f = f_ref[...]
    cs = cs_ref[...]
    gd = FNET_GROUP_DIM
    for j in range(kb):
        ar, ai, tr, ti = ar_ref[j], ai_ref[j], tr_ref[j], ti_ref[j]
        br = ar * tr - ai * ti
        bi = ar * ti + ai * tr
        v = jnp.dot(f, _data_rhs3(jnp.concatenate([br, bi], 0)), preferred_element_type=jnp.float32)
        vr, vi = v[:n2], v[n2:]
        for g in range(FNET_GROUPS):
            lhs = jnp.concatenate([vr[:, g * gd:(g + 1) * gd], vi[:, g * gd:(g + 1) * gd]], 1)
            y = jnp.dot(_data_lhs3(lhs), cs, preferred_element_type=jnp.float32)
            o_ref[:, j * CH + g * gd:j * CH + (g + 1) * gd] = y.astype(o_ref.dtype)


def fn_slab_call(a3, cst, name):
    n1, n2 = cst["n1"], cst["n2"]
    groups = a3.shape[0]
    kb = 8
    a5 = a3.reshape(groups, 2, n1, n2, CH)
    slab = lambda part: pl.BlockSpec((None, None, kb, n2, CH), lambda b, g: (g, part, b, 0, 0))
    tw = pl.BlockSpec((kb, n2, 1), lambda b, g: (b, 0, 0))
    return pl.pallas_call(
        functools.partial(_fn_slab_kernel, kb=kb, n2=n2),
        out_shape=jax.ShapeDtypeStruct((groups, n2, n1 * CH), jnp.bfloat16),
        grid=(n1 // kb, groups),
        in_specs=[slab(0), slab(1), tw, tw,
                  pl.BlockSpec(cst["f2"].shape, lambda b, g: (0, 0)),
                  pl.BlockSpec(cst["chan"].shape, lambda b, g: (0, 0))],
        out_specs=pl.BlockSpec((None, n2, kb * CH), lambda b, g: (g, 0, b)),
        compiler_params=_cparams(2),
        name=name,
    )(a5, a5, cst["tw_r"], cst["tw_i"], cst["f2"], cst["chan"])


def hyena_long(z, x0, n, batch, row_off, cst, tabs, filt_params, d_bias, tag):
    n1, n2 = cst["n1"], cst["n2"]
    w1, b1, w2, b2, w3, b3, w4, b4, freq = filt_params
    w1p = jnp.concatenate([w1, jnp.zeros((LANE - HYENA_EMB, HYENA_FILTER_HIDDEN), w1.dtype)], 0)
    circ, ssum = filter_call(n, tabs, w1p, b1[None, :], w2, b2[None, :], w3, b3[None, :],
                             w4, b4[None, :], freq)
    width = n2 * CH
    h3 = lead_call(cst["lead_real"], circ.reshape(n1, width), n1, 0, 1, "hyena_filt_dft1_" + tag)
    h5 = hy_slab_call(h3, cst, None, "hyena_filt_dft2_" + tag)
    z2d = z.reshape(-1, width)
    x02d = x0.reshape(-1, width)
    pairs = batch // 2
    a3 = lead_call(cst["lead_pair"], z2d, n1, row_off // n2, pairs, "hyena_dft1_" + tag)
    d5 = hy_slab_call(a3, cst, h5, "hyena_conv_" + tag)
    reps = width // CH if width <= 8192 else 8192 // CH
    y = lead_out_call(cst["lead_inv"], d5.reshape(pairs, 2 * n1, width), z2d, x02d,
                      jnp.tile(d_bias[None, :], (1, reps)), jnp.tile(ssum, (1, reps)),
                      n1, row_off // n2, "hyena_idft1_" + tag)
    return y.reshape(batch * n, CH)


def fnet_seq(u, n, batch, row_off, cst, tag):
    n1, n2 = cst["n1"], cst["n2"]
    width = n2 * CH
    a3 = lead_call(cst["lead_real"], u.reshape(-1, width), n1, row_off // n2, batch, "fnet_dft1_" + tag)
    return fn_slab_call(a3, cst, "fnet_dft2_" + tag).reshape(batch * n, CH)


SC_TM = CTX_LEN


def _sconv_kernel(u_ref, p_ref, n_ref, w_ref, b_ref, z_ref, x0_ref, *, lat_tiles, tiles_per_seq):
    i = pl.program_id(0)
    pos = i % tiles_per_seq
    is_lat = i < lat_tiles
    keep_prev = jnp.where(is_lat & (pos != 0), 1.0, 0.0)
    keep_next = jnp.where(is_lat & (pos != tiles_per_seq - 1), 1.0, 0.0)
    u = u_ref[...]
    tm = u.shape[0]
    row = lax.broadcasted_iota(jnp.int32, u.shape, 0)
    um = jnp.where(row == 0, p_ref[7:8, :] * keep_prev, pltpu.roll(u, 1, 0))
    up = jnp.where(row == tm - 1, n_ref[0:1, :] * keep_next, pltpu.roll(u, tm - 1, 0))
    w = w_ref[...]
    uc = b_ref[...] + um * w[0:1] + u * w[1:2] + up * w[2:3]
    x0_ref[...] = uc[:, :CH]
    z_ref[...] = uc[:, CH:2 * CH] * uc[:, 2 * CH:]


def sconv_call(u, conv_w, conv_b, l, n_lat, seq):
    rows = u.shape[0]
    tm = SC_TM
    per8 = tm // 8
    last8 = rows // 8 - 1
    out = jax.ShapeDtypeStruct((rows, CH), jnp.float32)
    return pl.pallas_call(
        functools.partial(_sconv_kernel, lat_tiles=n_lat // tm, tiles_per_seq=seq // tm),
        out_shape=(out, out),
        grid=(rows // tm,),
        in_specs=[pl.BlockSpec((tm, 3 * CH), lambda i: (i, 0)),
                  pl.BlockSpec((8, 3 * CH), lambda i: (jnp.maximum(i * per8 - 1, 0), 0)),
                  pl.BlockSpec((8, 3 * CH), lambda i: (jnp.minimum((i + 1) * per8, last8), 0)),
                  pl.BlockSpec((None, 3, 3 * CH), lambda i: (l, 0, 0)),
                  pl.BlockSpec((None, 1, 3 * CH), lambda i: (l, 0, 0))],
        out_specs=(pl.BlockSpec((tm, CH), lambda i: (i, 0)), pl.BlockSpec((tm, CH), lambda i: (i, 0))),
        compiler_params=_cparams(1),
        name="hyena_short_conv",
    )(u, u, u, conv_w, conv_b)


RANK_TM = 512


def _expert_onehots(r):
    lane = lax.broadcasted_iota(jnp.int32, r.shape, 1).astype(jnp.float32)
    return (jnp.where(lane == r[:, 0:1], 1.0, 0.0), jnp.where(lane == r[:, 1:2], 1.0, 0.0))


def _lanes01(v0, v1, shape):
    lane = lax.broadcasted_iota(jnp.int32, shape, 1)
    return jnp.where(lane == 0, v0, jnp.where(lane == 1, v1, 0.0))


def _rank_kernel(r_ref, o_ref, c_ref, cnt_ref):
    @pl.when(pl.program_id(0) == 0)
    def _():
        cnt_ref[...] = jnp.zeros_like(cnt_ref)

    r = r_ref[...]
    tm = r.shape[0]
    oh0, oh1 = _expert_onehots(r)
    earlier = (lax.broadcasted_iota(jnp.int32, (tm, tm), 1)
               < lax.broadcasted_iota(jnp.int32, (tm, tm), 0))
    tri = jnp.where(earlier, 1.0, 0.0).astype(jnp.bfloat16)
    pre0 = jnp.dot(tri, oh0.astype(jnp.bfloat16), preferred_element_type=jnp.float32)
    pre1 = jnp.dot(tri, oh1.astype(jnp.bfloat16), preferred_element_type=jnp.float32)
    cnt = cnt_ref[...]
    tot0 = jnp.sum(oh0, 0, keepdims=True)
    rank0 = jnp.sum((pre0 + cnt) * oh0, -1, keepdims=True)
    rank1 = jnp.sum((pre1 + cnt + tot0) * oh1, -1, keepdims=True)
    o_ref[...] = _lanes01(rank0, rank1, r.shape)
    cnt_ref[...] = cnt + tot0 + jnp.sum(oh1, 0, keepdims=True)
    c_ref[...] = cnt_ref[...]


def _dest_kernel(r_ref, k_ref, ps_ref, o_ref):
    r = r_ref[...]
    k = k_ref[...]
    oh0, oh1 = _expert_onehots(r)
    ps = ps_ref[...]
    d0 = jnp.sum(oh0 * ps, -1, keepdims=True) + k[:, 0:1]
    d1 = jnp.sum(oh1 * ps, -1, keepdims=True) + k[:, 1:2]
    o_ref[...] = _lanes01(d0, d1, r.shape).astype(jnp.int32)


def _dispatch(route):
    tm = RANK_TM
    row = pl.BlockSpec((tm, ROUTE_LANES), lambda i: (i, 0))
    one = pl.BlockSpec((1, ROUTE_LANES), lambda i: (0, 0))
    rank, counts = pl.pallas_call(
        _rank_kernel,
        out_shape=(jax.ShapeDtypeStruct((NT, ROUTE_LANES), jnp.float32),
                   jax.ShapeDtypeStruct((1, ROUTE_LANES), jnp.float32)),
        grid=(NT // tm,),
        in_specs=[row],
        out_specs=(row, one),
        scratch_shapes=[pltpu.VMEM((1, ROUTE_LANES), jnp.float32)],
        compiler_params=_cparams(1),
        name="moe_rank",
    )(route)
    counts = counts[0].astype(jnp.int32)
    padded = ((counts + MOE_TB - 1) // MOE_TB) * MOE_TB
    pend = jnp.cumsum(padded)
    pstart = (pend - padded).astype(jnp.float32)[None, :]
    dest = pl.pallas_call(
        _dest_kernel,
        out_shape=jax.ShapeDtypeStruct((NT, ROUTE_LANES), jnp.int32),
        grid=(NT // tm,),
        in_specs=[row, row, one],
        out_specs=row,
        compiler_params=_cparams(1),
        name="moe_dest",
    )(route, rank, pstart)
    dest = dest[:, :TOP_K]
    out_row = (jnp.arange(TOP_K, dtype=jnp.int32)[None, :] * NT
               + jnp.arange(NT, dtype=jnp.int32)[:, None])
    slot_v = jnp.full((MOE_SLOTS,), -1, jnp.int32).at[dest.reshape(-1)].set(out_row.reshape(-1))
    is_pad = slot_v < 0
    slot_tok = jnp.where(is_pad, 0, slot_v % NT)
    pend_e, cnt_e = pend[:N_EXPERTS], counts[:N_EXPERTS]
    starts = jnp.arange(MOE_BLOCKS, dtype=jnp.int32) * MOE_TB
    block_e = jnp.minimum(jnp.sum((pend_e[None, :] <= starts[:, None]).astype(jnp.int32), 1),
                          N_EXPERTS - 1)
    real_upto = jnp.cumsum(cnt_e)
    pad_rank = (jnp.arange(MOE_SLOTS, dtype=jnp.int32).reshape(MOE_BLOCKS, MOE_TB)
                - real_upto[block_e][:, None]).reshape(MOE_SLOTS)
    slot_dst = jnp.where(is_pad, NT * TOP_K + pad_rank, slot_v)
    n_used = (pend[N_EXPERTS - 1] // MOE_TB).astype(jnp.int32).reshape(1)
    return (slot_tok.reshape(MOE_BLOCKS, 1, MOE_TB), slot_dst.reshape(MOE_BLOCKS, 1, MOE_TB),
            block_e, n_used)


def kernel(x, c, ctx, c_ctx, w_ada, b_ada, w_in, conv_w, conv_b, hy_w1, hy_b1, hy_w2, hy_b2, hy_w3, hy_b3, hy_w4, hy_b4, hy_freq, hy_dbias, attn_sink, w_br_attn, w_br_hyena, w_br_fnet, w_out, ln1_g, ln1_b, ln2_g, ln2_b, rg_w, rg_b, re_w, re_b, moe_w1, moe_w3, moe_w2):
    bf16 = jnp.bfloat16
    xs = jnp.concatenate([x.reshape(N_LAT, D_MODEL), ctx.reshape(N_CTX, D_MODEL)], 0)
    c8 = jnp.concatenate([c, c_ctx[None, :], jnp.zeros((MOD_ROWS - BATCH - 1, D_MODEL), c.dtype)], 0)
    mod4 = ada_call(c8, w_ada, b_ada).reshape(DEPTH, MOD_ROWS, 1, 6 * D_MODEL)
    cos_t, sin_t = rope_tables(512)
    ln1_g3, ln1_b3 = ln1_g.reshape(DEPTH, 1, D_MODEL), ln1_b.reshape(DEPTH, 1, D_MODEL)
    ln2_g3, ln2_b3 = ln2_g.reshape(DEPTH, 1, D_MODEL), ln2_b.reshape(DEPTH, 1, D_MODEL)
    pad = jnp.zeros((DEPTH, D_MODEL, ROUTE_LANES - N_GROUPS - N_EXPERTS), jnp.float32)
    w_route = jnp.concatenate([rg_w, re_w, pad], -1).astype(bf16)
    b_route = jnp.concatenate([rg_b, re_b, pad[:, 0]], -1).reshape(DEPTH, 1, ROUTE_LANES)

    hy_lat, hy_ctx = dft_consts(2 * SEQ, 64), dft_consts(2 * CTX_LEN, 16)
    fn_lat, fn_ctx = fnet_consts(SEQ, 64), fnet_consts(CTX_LEN, 16)
    tab_lat, tab_ctx = filter_tables(SEQ), filter_tables(CTX_LEN)
    conv_b3 = conv_b.reshape(DEPTH, 1, 3 * HYENA_WIDTH)

    h = pre_call(xs, mod4)
    for l in range(DEPTH):
        qkv = qkv_call(h, w_in, l, cos_t, sin_t)
        u_hy = proj_call(h, w_in, l, HY_OFF, 3 * HYENA_WIDTH, None, jnp.float32, "proj_hyena")
        u_fn = proj_call(h, w_in, l, FN_OFF, FNET_WIDTH, None, jnp.float32, "proj_fnet")
        gates = proj_call(h, w_in, l, GATE_OFF, 3 * D_MODEL, "sigmoid", bf16, "proj_gates")
        a = attn_call(qkv, attn_sink, l)
        filt = (hy_w1[l], hy_b1[l], hy_w2[l], hy_b2[l], hy_w3[l], hy_b3[l], hy_w4[l], hy_b4[l], hy_freq[l])
        z, x0 = sconv_call(u_hy, conv_w, conv_b3, l, N_LAT, SEQ)
        y_hy = jnp.concatenate(
            [hyena_long(z, x0, SEQ, BATCH, 0, hy_lat, tab_lat, filt, hy_dbias[l], "lat"),
             hyena_long(z, x0, CTX_LEN, BATCH, N_LAT, hy_ctx, tab_ctx, filt, hy_dbias[l], "ctx")], 0)
        y_fn = jnp.concatenate([fnet_seq(u_fn, SEQ, BATCH, 0, fn_lat, "lat"),
                                fnet_seq(u_fn, CTX_LEN, BATCH, N_LAT, fn_ctx, "ctx")], 0)
        xs, t = mix_call(a, y_hy, y_fn, gates, xs,
                         w_br_attn[l].astype(bf16), w_br_hyena[l].astype(bf16),
                         w_br_fnet[l].astype(bf16), w_out[l].astype(bf16),
                         mod4, ln1_g3, ln1_b3, l)
        route = route_call(t, w_route[l], b_route[l])
        slot_tok, slot_dst, block_e, n_used = _dispatch(route)
        yk = moe_call(block_e, n_used, slot_tok, slot_dst, t, moe_w1, moe_w3, moe_w2, l)
        if l < DEPTH - 1:
            xs, h = post_call(xs, yk, route, mod4, ln2_g3, ln2_b3, l)
    return post_call(xs, yk, route, mod4, ln2_g3, ln2_b3, DEPTH - 1).reshape(BATCH, SEQ, D_MODEL)
```

```python
import functools
import math

import numpy as np
import jax
import jax.numpy as jnp
from jax import lax
from jax.experimental import pallas as pl
from jax.experimental.pallas import tpu as pltpu

D_MODEL = 2048
BATCH = 4
SEQ = 4096
DEPTH = 4
GRID_W = 64
CTX_LEN = 256

N_HEADS = 8
N_KV_HEADS = 2
HEAD_DIM = 128
Q_GROUP = N_HEADS // N_KV_HEADS
ATTN_WIDTH = N_HEADS * HEAD_DIM
KV_WIDTH = N_KV_HEADS * HEAD_DIM
WINDOW = 128
ROPE_THETA = 10000.0
MASK_VALUE = -1e30

HYENA_WIDTH = 512
HYENA_SHORT = 3
HYENA_BANDS = 16
HYENA_EMB = 1 + 2 * HYENA_BANDS
HYENA_FILTER_HIDDEN = 64
HYENA_TARGET = 1e-2
HYENA_MAX_DECAY = math.log(HYENA_TARGET) / 0.3
HYENA_MIN_DECAY = math.log(HYENA_TARGET) / 1.5

FNET_WIDTH = 512
FNET_GROUPS = 4
FNET_GROUP_DIM = FNET_WIDTH // FNET_GROUPS

Q_OFF = 0
K_OFF = Q_OFF + ATTN_WIDTH
V_OFF = K_OFF + KV_WIDTH
HY_OFF = V_OFF + KV_WIDTH
FN_OFF = HY_OFF + 3 * HYENA_WIDTH
GATE_OFF = FN_OFF + FNET_WIDTH
IN_WIDTH = GATE_OFF + 3 * D_MODEL
QKV_WIDTH = HY_OFF

N_GROUPS = 4
EXPERTS_PER_GROUP = 8
N_EXPERTS = N_GROUPS * EXPERTS_PER_GROUP
TOP_K = 2
EXPERT_HIDDEN = 512

LN_EPS = 1e-6
DEEPNORM_ALPHA = (2 * DEPTH) ** 0.25

N_LAT = BATCH * SEQ
N_CTX = BATCH * CTX_LEN
NT = N_LAT + N_CTX
MOD_ROWS = 8

LANE = 128
VMEM_LIMIT = 56 * 1024 * 1024

MOE_TB = 256
MOE_BLOCKS = -(-(NT * TOP_K + N_EXPERTS * (MOE_TB - 1)) // MOE_TB)
MOE_SLOTS = MOE_BLOCKS * MOE_TB


def _cparams(n_axes):
    return pltpu.CompilerParams(dimension_semantics=("arbitrary",) * n_axes,
                                vmem_limit_bytes=VMEM_LIMIT)


def _mod_row(i, tm):
    return jnp.minimum((i * tm) // SEQ, BATCH)


def _mod_spec(l, chunk, tm, axis=0):
    def imap(*g):
        return (l, _mod_row(g[axis], tm), 0, chunk)
    return pl.BlockSpec((None, None, 1, D_MODEL), imap)


def _ln(x):
    mu = jnp.mean(x, -1, keepdims=True)
    xc = x - mu
    var = jnp.mean(xc * xc, -1, keepdims=True)
    return xc * lax.rsqrt(var + LN_EPS)


def _ada_kernel(c_ref, w_ref, b_ref, o_ref):
    c = c_ref[...]
    a = (c * jax.nn.sigmoid(c)).astype(jnp.bfloat16)
    o_ref[...] = jnp.dot(a, w_ref[...].astype(jnp.bfloat16),
                         preferred_element_type=jnp.float32) + b_ref[...]


def ada_call(c8, w_ada, b_ada):
    tn = 1024
    n = 6 * D_MODEL
    return pl.pallas_call(
        _ada_kernel,
        out_shape=jax.ShapeDtypeStruct((DEPTH, MOD_ROWS, n), jnp.float32),
        grid=(DEPTH, n // tn),
        in_specs=[pl.BlockSpec((MOD_ROWS, D_MODEL), lambda l, j: (0, 0)),
                  pl.BlockSpec((None, D_MODEL, tn), lambda l, j: (l, 0, j)),
                  pl.BlockSpec((None, 1, tn), lambda l, j: (l, 0, j))],
        out_specs=pl.BlockSpec((None, MOD_ROWS, tn), lambda l, j: (l, 0, j)),
        compiler_params=_cparams(2),
        name="adaln",
    )(c8, w_ada, b_ada.reshape(DEPTH, 1, n))


def _pre_kernel(x_ref, sh_ref, sc_ref, h_ref):
    h_ref[...] = (_ln(x_ref[...]) * (1.0 + sc_ref[...]) + sh_ref[...]).astype(h_ref.dtype)


def pre_call(x, mod4):
    tm = 512
    return pl.pallas_call(
        _pre_kernel,
        out_shape=jax.ShapeDtypeStruct((NT, D_MODEL), jnp.bfloat16),
        grid=(NT // tm,),
        in_specs=[pl.BlockSpec((tm, D_MODEL), lambda i: (i, 0)),
                  _mod_spec(0, 0, tm), _mod_spec(0, 1, tm)],
        out_specs=pl.BlockSpec((tm, D_MODEL), lambda i: (i, 0)),
        compiler_params=_cparams(1),
        name="pre_modulate",
    )(x, mod4, mod4)


def _proj_kernel(x_ref, w_ref, o_ref, wb_ref, *, act):
    @pl.when(pl.program_id(1) == 0)
    def _():
        wb_ref[...] = w_ref[...].astype(jnp.bfloat16)

    acc = jnp.dot(x_ref[...], wb_ref[...], preferred_element_type=jnp.float32)
    if act == "sigmoid":
        acc = jax.nn.sigmoid(acc)
    o_ref[...] = acc.astype(o_ref.dtype)


def proj_call(h, w_in, l, col_off, width, act, out_dtype, name):
    tm, tn = 1024, 512
    off = col_off // tn
    return pl.pallas_call(
        functools.partial(_proj_kernel, act=act),
        out_shape=jax.ShapeDtypeStruct((NT, width), out_dtype),
        grid=(width // tn, NT // tm),
        in_specs=[pl.BlockSpec((tm, D_MODEL), lambda j, i: (i, 0)),
                  pl.BlockSpec((None, D_MODEL, tn), lambda j, i: (l, 0, off + j))],
        out_specs=pl.BlockSpec((tm, tn), lambda j, i: (i, j)),
        scratch_shapes=[pltpu.VMEM((D_MODEL, tn), jnp.bfloat16)],
        compiler_params=_cparams(2),
        name=name,
    )(h, w_in)


def _swap32(x):
    lane = lax.broadcasted_iota(jnp.int32, x.shape, 1)
    up = pltpu.roll(x, LANE - 32, 1)
    dn = pltpu.roll(x, 32, 1)
    return jnp.where((lane & 63) < 32, up, dn)


def _qkv_kernel(x_ref, w_ref, cos_ref, sin_ref, o_ref, wb_ref):
    @pl.when(pl.program_id(0) == 0)
    def _():
        wb_ref[...] = w_ref[...].astype(jnp.bfloat16)

    cos = cos_ref[...]
    sin = sin_ref[...]
    scale = HEAD_DIM ** -0.5
    for hd in range(QKV_WIDTH // HEAD_DIM):
        sl = slice(hd * HEAD_DIM, (hd + 1) * HEAD_DIM)
        acc = jnp.dot(x_ref[...], wb_ref[:, sl], preferred_element_type=jnp.float32)
        if hd < N_HEADS + N_KV_HEADS:
            acc = acc * cos + _swap32(acc) * sin
        if hd < N_HEADS:
            acc = acc * scale
        o_ref[:, sl] = acc.astype(o_ref.dtype)


def qkv_call(h, w_in, l, cos_t, sin_t):
    tm = 512
    n_pos = SEQ // tm

    def rope_map(i):
        return (jnp.where(i * tm < N_LAT, i % n_pos, n_pos), 0)

    return pl.pallas_call(
        _qkv_kernel,
        out_shape=jax.ShapeDtypeStruct((NT, QKV_WIDTH), jnp.bfloat16),
        grid=(NT // tm,),
        in_specs=[pl.BlockSpec((tm, D_MODEL), lambda i: (i, 0)),
                  pl.BlockSpec((None, D_MODEL, QKV_WIDTH), lambda i: (l, 0, 0),
                               pipeline_mode=pl.Buffered(1)),
                  pl.BlockSpec((tm, HEAD_DIM), rope_map),
                  pl.BlockSpec((tm, HEAD_DIM), rope_map)],
        out_specs=pl.BlockSpec((tm, QKV_WIDTH), lambda i: (i, 0)),
        scratch_shapes=[pltpu.VMEM((D_MODEL, QKV_WIDTH), jnp.bfloat16)],
        compiler_params=_cparams(1),
        name="qkv_rope",
    )(h, w_in, cos_t, sin_t)


def rope_tables(tm):
    n_freq = HEAD_DIM // 4
    inv_freq = ROPE_THETA ** (-np.arange(n_freq, dtype=np.float64) / n_freq)
    t = np.arange(SEQ)
    ang_row = (t // GRID_W)[:, None] * inv_freq[None, :]
    ang_col = (t % GRID_W)[:, None] * inv_freq[None, :]
    cos = np.concatenate([np.cos(ang_row), np.cos(ang_row), np.cos(ang_col), np.cos(ang_col)], 1)
    sin = np.concatenate([-np.sin(ang_row), np.sin(ang_row), -np.sin(ang_col), np.sin(ang_col)], 1)
    cos = np.concatenate([cos, np.ones((tm, HEAD_DIM))], 0)
    sin = np.concatenate([sin, np.zeros((tm, HEAD_DIM))], 0)
    return jnp.asarray(cos, jnp.float32), jnp.asarray(sin, jnp.float32)


ATT_QT = 256
ATT_LAT_TILES = N_LAT // ATT_QT
ATT_TILES_PER_SEQ = SEQ // ATT_QT


def _attn_kernel(sink_ref, q_ref, kp_ref, kc_ref, kn_ref, vp_ref, vc_ref, vn_ref,
                 kx_ref, vx_ref, o_ref, *, l):
    i = pl.program_id(0)
    h = pl.program_id(1)
    is_lat = i < ATT_LAT_TILES
    base = (i % ATT_TILES_PER_SEQ) * ATT_QT
    n_loc = ATT_QT + 2 * WINDOW
    r = lax.broadcasted_iota(jnp.int32, (ATT_QT, n_loc), 0)
    j = lax.broadcasted_iota(jnp.int32, (ATT_QT, n_loc), 1)
    kpos = base - WINDOW + j
    dist = j - r
    k_end = jnp.where(is_lat, SEQ, 0)
    valid = (dist >= 0) & (dist <= 2 * WINDOW) & (kpos >= 0) & (kpos < k_end)

    k_loc = jnp.concatenate([kp_ref[...], kc_ref[...], kn_ref[...]], axis=0)
    v_loc = jnp.concatenate([vp_ref[...], vc_ref[...], vn_ref[...]], axis=0)
    k_ctx = kx_ref[...]
    v_ctx = vx_ref[...]
    dn = (((1,), (1,)), ((), ()))
    for g in range(Q_GROUP):
        sl = slice(g * HEAD_DIM, (g + 1) * HEAD_DIM)
        q = q_ref[:, sl]
        s_loc = lax.dot_general(q, k_loc, dn, preferred_element_type=jnp.float32)
        s_loc = jnp.where(valid, s_loc, MASK_VALUE)
        s_ctx = lax.dot_general(q, k_ctx, dn, preferred_element_type=jnp.float32)
        sink = sink_ref[l, h * Q_GROUP + g]
        m = jnp.maximum(jnp.maximum(jnp.max(s_loc, -1, keepdims=True),
                                    jnp.max(s_ctx, -1, keepdims=True)), sink)
        e_loc = jnp.exp(s_loc - m)
        e_ctx = jnp.exp(s_ctx - m)
        den = (jnp.sum(e_loc, -1, keepdims=True) + jnp.sum(e_ctx, -1, keepdims=True)
               + jnp.exp(sink - m))
        o = (jnp.dot(e_loc.astype(jnp.bfloat16), v_loc, preferred_element_type=jnp.float32)
             + jnp.dot(e_ctx.astype(jnp.bfloat16), v_ctx, preferred_element_type=jnp.float32))
        o_ref[:, sl] = (o / den).astype(o_ref.dtype)


def attn_call(qkv, sink, l):
    blk = WINDOW
    per = ATT_QT // blk
    last_blk = NT // blk - 1
    kcol, vcol = K_OFF // HEAD_DIM, V_OFF // HEAD_DIM

    def batch_of(i):
        return jnp.where(i < ATT_LAT_TILES, i // ATT_TILES_PER_SEQ, i - ATT_LAT_TILES)

    def prev_map(col):
        return lambda i, h: (jnp.maximum(i * per - 1, 0), col + h)

    def next_map(col):
        return lambda i, h: (jnp.minimum(i * per + per, last_blk), col + h)

    def cur_map(col):
        return lambda i, h: (i, col + h)

    def ctx_map(col):
        return lambda i, h: (N_LAT // CTX_LEN + batch_of(i), col + h)

    return pl.pallas_call(
        functools.partial(_attn_kernel, l=l),
        out_shape=jax.ShapeDtypeStruct((NT, ATTN_WIDTH), jnp.bfloat16),
        grid=(NT // ATT_QT, N_KV_HEADS),
        in_specs=[pl.BlockSpec(memory_space=pltpu.SMEM),
                  pl.BlockSpec((ATT_QT, Q_GROUP * HEAD_DIM), lambda i, h: (i, h)),
                  pl.BlockSpec((blk, HEAD_DIM), prev_map(kcol)),
                  pl.BlockSpec((ATT_QT, HEAD_DIM), cur_map(kcol)),
                  pl.BlockSpec((blk, HEAD_DIM), next_map(kcol)),
                  pl.BlockSpec((blk, HEAD_DIM), prev_map(vcol)),
                  pl.BlockSpec((ATT_QT, HEAD_DIM), cur_map(vcol)),
                  pl.BlockSpec((blk, HEAD_DIM), next_map(vcol)),
                  pl.BlockSpec((CTX_LEN, HEAD_DIM), ctx_map(kcol)),
                  pl.BlockSpec((CTX_LEN, HEAD_DIM), ctx_map(vcol))],
        out_specs=pl.BlockSpec((ATT_QT, Q_GROUP * HEAD_DIM), lambda i, h: (i, h)),
        compiler_params=_cparams(2),
        name="window_attention",
    )(sink, qkv, qkv, qkv, qkv, qkv, qkv, qkv, qkv, qkv)


def _mix_kernel(a_ref, yh_ref, yf_ref, g_ref, x_ref, wa_ref, wh_ref, wf_ref, wo_ref,
                g1_ref, sh2_ref, sc2_ref, lg_ref, lb_ref, xo_ref, t_ref, y_ref):
    tn = 512
    for c in range(D_MODEL // tn):
        sl = slice(c * tn, (c + 1) * tn)
        ya = jnp.dot(a_ref[...], wa_ref[:, sl], preferred_element_type=jnp.float32)
        yh = jnp.dot(yh_ref[...], wh_ref[:, sl], preferred_element_type=jnp.float32)
        yf = jnp.dot(yf_ref[...], wf_ref[:, sl], preferred_element_type=jnp.float32)
        y = (g_ref[:, sl].astype(jnp.float32) * ya
             + g_ref[:, D_MODEL + c * tn:D_MODEL + (c + 1) * tn].astype(jnp.float32) * yh
             + g_ref[:, 2 * D_MODEL + c * tn:2 * D_MODEL + (c + 1) * tn].astype(jnp.float32) * yf)
        y_ref[:, sl] = y.astype(y_ref.dtype)
    o = jnp.dot(y_ref[...], wo_ref[...], preferred_element_type=jnp.float32)
    xn = _ln(DEEPNORM_ALPHA * x_ref[...] + g1_ref[...] * o) * lg_ref[...] + lb_ref[...]
    xo_ref[...] = xn
    t_ref[...] = (_ln(xn) * (1.0 + sc2_ref[...]) + sh2_ref[...]).astype(t_ref.dtype)


def mix_call(a, yh, yf, gates, x, wa, wh, wf, wo, mod4, ln_g, ln_b, l):
    tm = 256
    row = lambda i: (i, 0)
    full = lambda i: (0, 0)
    const = pl.Buffered(1)
    return pl.pallas_call(
        _mix_kernel,
        out_shape=(jax.ShapeDtypeStruct((NT, D_MODEL), jnp.float32),
                   jax.ShapeDtypeStruct((NT, D_MODEL), jnp.float32)),
        grid=(NT // tm,),
        in_specs=[pl.BlockSpec((tm, ATTN_WIDTH), row),
                  pl.BlockSpec((tm, HYENA_WIDTH), row),
                  pl.BlockSpec((tm, FNET_WIDTH), row),
                  pl.BlockSpec((tm, 3 * D_MODEL), row),
                  pl.BlockSpec((tm, D_MODEL), row),
                  pl.BlockSpec((ATTN_WIDTH, D_MODEL), full, pipeline_mode=const),
                  pl.BlockSpec((HYENA_WIDTH, D_MODEL), full, pipeline_mode=const),
                  pl.BlockSpec((FNET_WIDTH, D_MODEL), full, pipeline_mode=const),
                  pl.BlockSpec((D_MODEL, D_MODEL), full, pipeline_mode=const),
                  _mod_spec(l, 2, tm), _mod_spec(l, 3, tm), _mod_spec(l, 4, tm),
                  pl.BlockSpec((None, 1, D_MODEL), lambda i: (l, 0, 0)),
                  pl.BlockSpec((None, 1, D_MODEL), lambda i: (l, 0, 0))],
        out_specs=(pl.BlockSpec((tm, D_MODEL), row), pl.BlockSpec((tm, D_MODEL), row)),
        scratch_shapes=[pltpu.VMEM((tm, D_MODEL), jnp.bfloat16)],
        compiler_params=_cparams(1),
        name="mixer_out",
    )(a, yh, yf, gates, x, wa, wh, wf, wo, mod4, mod4, mod4, ln_g, ln_b)


ROUTE_LANES = LANE


def _route_kernel(t_ref, w_ref, b_ref, o_ref):
    logits = jnp.dot(t_ref[...].astype(jnp.bfloat16), w_ref[...],
                     preferred_element_type=jnp.float32) + b_ref[...]
    lane = lax.broadcasted_iota(jnp.int32, logits.shape, 1)
    neg = -jnp.inf
    big = jnp.int32(2 ** 30)

    def first_argmax(vals):
        mx = jnp.max(vals, -1, keepdims=True)
        idx = jnp.min(jnp.where(vals == mx, lane, big), -1, keepdims=True)
        return mx, idx

    g_vals = jnp.where(lane < N_GROUPS, logits, neg)
    g_max, grp = first_argmax(g_vals)
    g_w = 1.0 / jnp.sum(jnp.exp(g_vals - g_max), -1, keepdims=True)
    lo = N_GROUPS + grp * EXPERTS_PER_GROUP
    e_vals = jnp.where((lane >= lo) & (lane < lo + EXPERTS_PER_GROUP), logits, neg)
    v1, i1 = first_argmax(e_vals)
    v2, i2 = first_argmax(jnp.where(lane == i1, neg, e_vals))
    e2 = jnp.exp(v2 - v1)
    w1 = g_w / (1.0 + e2)
    w2 = g_w * e2 / (1.0 + e2)
    id1 = (i1 - N_GROUPS).astype(jnp.float32)
    id2 = (i2 - N_GROUPS).astype(jnp.float32)
    o_ref[...] = jnp.where(lane == 0, id1, jnp.where(lane == 1, id2,
                           jnp.where(lane == 2, w1, jnp.where(lane == 3, w2, 0.0))))


def route_call(t, wr, br):
    tm = 512
    return pl.pallas_call(
        _route_kernel,
        out_shape=jax.ShapeDtypeStruct((NT, ROUTE_LANES), jnp.float32),
        grid=(NT // tm,),
        in_specs=[pl.BlockSpec((tm, D_MODEL), lambda i: (i, 0)),
                  pl.BlockSpec((D_MODEL, ROUTE_LANES), lambda i: (0, 0)),
                  pl.BlockSpec((1, ROUTE_LANES), lambda i: (0, 0))],
        out_specs=pl.BlockSpec((tm, ROUTE_LANES), lambda i: (i, 0)),
        compiler_params=_cparams(1),
        name="moe_router",
    )(t, wr, br)


DISP_TM = 256


def _wait_rows(src, dst, sem):
    pltpu.make_async_copy(src, dst, sem).wait()


def _disp_kernel(d_ref, t_hbm, xin_hbm, x_hbm, tbuf, lsem, ssem):
    del xin_hbm
    i = pl.program_id(0)
    n = pl.num_programs(0)

    def load(tile, s):
        return pltpu.make_async_copy(t_hbm.at[pl.ds(tile * DISP_TM, DISP_TM)], tbuf.at[s], lsem.at[s])

    def wait_scatter(s):
        for _k in range(TOP_K):
            _wait_rows(tbuf.at[s], x_hbm.at[pl.ds(0, DISP_TM)], ssem.at[s])

    @pl.when(i == 0)
    def _():
        load(0, 0).start()

    def step(slot):
        load(i, slot).wait()

        @pl.when(i >= 1)
        def _():
            wait_scatter(1 - slot)

        @pl.when(i + 1 < n)
        def _():
            load(i + 1, 1 - slot).start()

        for k in range(TOP_K):
            for r in range(DISP_TM):
                pltpu.make_async_copy(tbuf.at[slot, pl.ds(r, 1)], x_hbm.at[pl.ds(d_ref[k, r], 1)],
                                      ssem.at[slot]).start()

        @pl.when(i == n - 1)
        def _():
            wait_scatter(slot)

    for parity in range(2):
        pl.when(i % 2 == parity)(functools.partial(step, parity))


def disp_call(dest_t, t, xb):
    return pl.pallas_call(
        _disp_kernel,
        out_shape=jax.ShapeDtypeStruct((MOE_SLOTS, D_MODEL), jnp.float32),
        grid=(NT // DISP_TM,),
        in_specs=[pl.BlockSpec((None, TOP_K, DISP_TM), lambda i: (i, 0, 0), memory_space=pltpu.SMEM),
                  pl.BlockSpec(memory_space=pl.ANY),
                  pl.BlockSpec(memory_space=pl.ANY)],
        out_specs=pl.BlockSpec(memory_space=pl.ANY),
        scratch_shapes=[pltpu.VMEM((2, DISP_TM, D_MODEL), jnp.float32),
                        pltpu.SemaphoreType.DMA((2,)),
                        pltpu.SemaphoreType.DMA((2,))],
        input_output_aliases={2: 0},
        compiler_params=_cparams(1),
        name="moe_dispatch",
    )(dest_t, t, xb)


def _moe_kernel(be_ref, nu_ref, x_ref, w1_ref, w3_ref, w2_ref, o_ref, b1_ref, b3_ref, b2_ref):
    i = pl.program_id(0)
    first = (i == 0) | (be_ref[i] != be_ref[jnp.maximum(i - 1, 0)])

    @pl.when(first)
    def _():
        b1_ref[...] = w1_ref[...].astype(jnp.bfloat16)
        b3_ref[...] = w3_ref[...].astype(jnp.bfloat16)
        b2_ref[...] = w2_ref[...].astype(jnp.bfloat16)

    @pl.when(i < nu_ref[0])
    def _():
        x = x_ref[...].astype(jnp.bfloat16)
        h1 = jnp.dot(x, b1_ref[...], preferred_element_type=jnp.float32)
        h3 = jnp.dot(x, b3_ref[...], preferred_element_type=jnp.float32)
        hh = (h1 * jax.nn.sigmoid(h1) * h3).astype(jnp.bfloat16)
        o_ref[...] = jnp.dot(hh, b2_ref[...], preferred_element_type=jnp.float32)

    @pl.when(i >= nu_ref[0])
    def _():
        o_ref[...] = jnp.zeros_like(o_ref)


def moe_call(block_e, n_used, xb, w1, w3, w2, l):
    grid_spec = pltpu.PrefetchScalarGridSpec(
        num_scalar_prefetch=2,
        grid=(MOE_BLOCKS,),
        in_specs=[pl.BlockSpec((MOE_TB, D_MODEL), lambda i, be, nu: (i, 0)),
                  pl.BlockSpec((None, None, D_MODEL, EXPERT_HIDDEN), lambda i, be, nu: (l, be[i], 0, 0)),
                  pl.BlockSpec((None, None, D_MODEL, EXPERT_HIDDEN), lambda i, be, nu: (l, be[i], 0, 0)),
                  pl.BlockSpec((None, None, EXPERT_HIDDEN, D_MODEL), lambda i, be, nu: (l, be[i], 0, 0))],
        out_specs=pl.BlockSpec((MOE_TB, D_MODEL), lambda i, be, nu: (i, 0)),
        scratch_shapes=[pltpu.VMEM((D_MODEL, EXPERT_HIDDEN), jnp.bfloat16),
                        pltpu.VMEM((D_MODEL, EXPERT_HIDDEN), jnp.bfloat16),
                        pltpu.VMEM((EXPERT_HIDDEN, D_MODEL), jnp.bfloat16)])
    return pl.pallas_call(
        _moe_kernel,
        out_shape=jax.ShapeDtypeStruct((MOE_SLOTS, D_MODEL), jnp.float32),
        grid_spec=grid_spec,
        compiler_params=_cparams(1),
        name="moe_experts",
    )(block_e, n_used, xb, w1, w3, w2)


def _post_kernel(d_ref, dn_ref, x_ref, r_ref, g2_ref, lg_ref, lb_ref, *rest, last):
    if last:
        y_hbm, xo_ref, ybuf, sem = rest
    else:
        sh_ref, sc_ref, y_hbm, xo_ref, h_ref, ybuf, sem = rest
    i = pl.program_id(0)
    n = pl.num_programs(0)

    def gather(idx_ref, s):
        for k in range(TOP_K):
            for r in range(DISP_TM):
                pltpu.make_async_copy(y_hbm.at[pl.ds(idx_ref[k, r], 1)], ybuf.at[s, k, pl.ds(r, 1)],
                                      sem.at[s]).start()

    @pl.when(i == 0)
    def _():
        gather(d_ref, 0)

    def step(slot):
        for k in range(TOP_K):
            _wait_rows(y_hbm.at[pl.ds(0, DISP_TM)], ybuf.at[slot, k], sem.at[slot])

        @pl.when(i + 1 < n)
        def _():
            gather(dn_ref, 1 - slot)

        r = r_ref[...]
        m = r[:, 2:3] * ybuf[slot, 0] + r[:, 3:4] * ybuf[slot, 1]
        xn = _ln(DEEPNORM_ALPHA * x_ref[...] + g2_ref[...] * m) * lg_ref[...] + lb_ref[...]
        xo_ref[...] = xn
        if not last:
            h_ref[...] = (_ln(xn) * (1.0 + sc_ref[...]) + sh_ref[...]).astype(h_ref.dtype)

    for parity in range(2):
        pl.when(i % 2 == parity)(functools.partial(step, parity))


def post_call(x, yb, dest_t, route, mod4, ln_g, ln_b, l):
    tm = DISP_TM
    last = l == DEPTH - 1
    rows = N_LAT if last else NT
    n_tiles = rows // tm
    row = lambda i: (i, 0)
    x_spec = pl.BlockSpec((tm, D_MODEL), row)
    smem = lambda f: pl.BlockSpec((None, TOP_K, tm), f, memory_space=pltpu.SMEM)
    in_specs = [smem(lambda i: (i, 0, 0)),
                smem(lambda i: (jnp.minimum(i + 1, n_tiles - 1), 0, 0)),
                x_spec,
                pl.BlockSpec((tm, ROUTE_LANES), row),
                _mod_spec(l, 5, tm),
                pl.BlockSpec((None, 1, D_MODEL), lambda i: (l, 0, 0)),
                pl.BlockSpec((None, 1, D_MODEL), lambda i: (l, 0, 0))]
    args = [dest_t, dest_t, x, route, mod4, ln_g, ln_b]
    x_out = jax.ShapeDtypeStruct((rows, D_MODEL), jnp.float32)
    if last:
        out_shape, out_specs = x_out, x_spec
    else:
        in_specs += [_mod_spec(l + 1, 0, tm), _mod_spec(l + 1, 1, tm)]
        args += [mod4, mod4]
        out_shape = (x_out, jax.ShapeDtypeStruct((rows, D_MODEL), jnp.bfloat16))
        out_specs = (x_spec, x_spec)
    in_specs.append(pl.BlockSpec(memory_space=pl.ANY))
    args.append(yb)
    return pl.pallas_call(
        functools.partial(_post_kernel, last=last),
        out_shape=out_shape,
        grid=(n_tiles,),
        in_specs=in_specs,
        out_specs=out_specs,
        scratch_shapes=[pltpu.VMEM((2, TOP_K, tm, D_MODEL), jnp.float32),
                        pltpu.SemaphoreType.DMA((2,))],
        compiler_params=_cparams(1),
        name="moe_combine_postln",
    )(*args)


CH = HYENA_WIDTH


def _hi_lo(x):
    hi = x.astype(jnp.bfloat16)
    lo = (x - hi.astype(jnp.float32)).astype(jnp.bfloat16)
    return hi, lo


def _const_lhs3(m):
    hi, lo = _hi_lo(jnp.asarray(m, jnp.float32))
    return jnp.concatenate([hi, lo, hi], 1)


def _data_rhs3(x):
    hi, lo = _hi_lo(x)
    return jnp.concatenate([hi, hi, lo], 0)


def _const_rhs3(m):
    hi, lo = _hi_lo(jnp.asarray(m, jnp.float32))
    return jnp.concatenate([hi, hi, lo], 0)


def _data_lhs3(x):
    hi, lo = _hi_lo(x)
    return jnp.concatenate([hi, lo, hi], 1)


def _cs(n, rows, cols):
    ang = 2.0 * np.pi * np.outer(np.arange(rows), np.arange(cols)) / n
    return np.cos(ang), np.sin(ang)


def dft_consts(n_fft, n1):
    n2 = n_fft // n1
    c1, s1 = _cs(n1, n1, n1)
    c2, s2 = _cs(n2, n2, n2)
    h = n1 // 2
    tr, ti = _cs(n_fft, n1, n2)
    return dict(
        n1=n1, n2=n2,
        lead_real=_const_lhs3(np.concatenate([c1, -s1], 0)),
        lead_pair=_const_lhs3(np.block([[c1[:, :h], s1[:, :h]], [-s1[:, :h], c1[:, :h]]])),
        lead_inv=_const_lhs3(np.block([[c1[:h, :], -s1[:h, :]], [s1[:h, :], c1[:h, :]]]) / n_fft),
        f2=_const_lhs3(np.block([[c2, s2], [-s2, c2]])),
        f2_inv=_const_lhs3(np.block([[c2, -s2], [s2, c2]])),
        tw_r=jnp.asarray(tr[:, :, None], jnp.float32),
        tw_i=jnp.asarray(-ti[:, :, None], jnp.float32),
    )


def _lead_kernel(m_ref, x_ref, o_ref):
    o_ref[...] = jnp.dot(m_ref[...], _data_rhs3(x_ref[...]), preferred_element_type=jnp.float32)


def lead_call(mcat, x2d, k, row_off, groups, name):
    mo = mcat.shape[0]
    width = x2d.shape[1]
    tn = min(width, 8192)
    off = row_off // k
    return pl.pallas_call(
        _lead_kernel,
        out_shape=jax.ShapeDtypeStruct((groups, mo, width), jnp.float32),
        grid=(groups, width // tn),
        in_specs=[pl.BlockSpec(mcat.shape, lambda g, j: (0, 0)),
                  pl.BlockSpec((k, tn), lambda g, j: (off + g, j))],
        out_specs=pl.BlockSpec((None, mo, tn), lambda g, j: (g, 0, j)),
        compiler_params=_cparams(2),
        name=name,
    )(mcat, x2d)


def _lead_out_kernel(m_ref, d_ref, z_ref, x0_ref, db_ref, s_ref, o_ref):
    acc = jnp.dot(m_ref[...], _data_rhs3(d_ref[...]), preferred_element_type=jnp.float32)
    y = acc * (1.0 / (s_ref[...] + 1e-6)) + z_ref[...] * db_ref[...]
    o_ref[...] = (x0_ref[...] * y).astype(o_ref.dtype)


def lead_out_call(mcat, d3, z2d, x02d, dbias_t, ssum_t, k_out, row_off, name):
    groups, k_in, width = d3.shape
    tn = min(width, 8192)
    off = row_off // k_out
    view = pl.BlockSpec((k_out, tn), lambda g, j: (off + g, j))
    lanes = pl.BlockSpec((1, tn), lambda g, j: (0, 0))
    return pl.pallas_call(
        _lead_out_kernel,
        out_shape=jax.ShapeDtypeStruct((groups, k_out, width), jnp.bfloat16),
        grid=(groups, width // tn),
        in_specs=[pl.BlockSpec(mcat.shape, lambda g, j: (0, 0)),
                  pl.BlockSpec((None, k_in, tn), lambda g, j: (g, 0, j)),
                  view, view, lanes, lanes],
        out_specs=pl.BlockSpec((None, k_out, tn), lambda g, j: (g, 0, j)),
        compiler_params=_cparams(2),
        name=name,
    )(mcat, d3, z2d, x02d, dbias_t, ssum_t)


def _hy_slab_kernel(*refs, kb, n2, conv):
    if conv:
        ar_ref, ai_ref, tr_ref, ti_ref, f_ref, fi_ref, hr_ref, hi_ref, o_ref = refs
    else:
        ar_ref, ai_ref, tr_ref, ti_ref, f_ref, o_ref = refs
    f = f_ref[...]
    for j in range(kb):
        ar, ai, tr, ti = ar_ref[j], ai_ref[j], tr_ref[j], ti_ref[j]
        br = ar * tr - ai * ti
        bi = ar * ti + ai * tr
        x = jnp.dot(f, _data_rhs3(jnp.concatenate([br, bi], 0)), preferred_element_type=jnp.float32)
        xr, xi = x[:n2], x[n2:]
        if conv:
            hr, hi = hr_ref[j], hi_ref[j]
            yr = xr * hr - xi * hi
            yi = xr * hi + xi * hr
            d = jnp.dot(fi_ref[...], _data_rhs3(jnp.concatenate([yr, yi], 0)),
                        preferred_element_type=jnp.float32)
            dr, di = d[:n2], d[n2:]
            o_ref[0, j] = dr * tr + di * ti
            o_ref[1, j] = di * tr - dr * ti
        else:
            o_ref[0, j] = xr
            o_ref[1, j] = xi


def hy_slab_call(a3, cst, h5, name):
    n1, n2 = cst["n1"], cst["n2"]
    groups = a3.shape[0]
    kb = 8
    a5 = a3.reshape(groups, 2, n1, n2, CH)
    slab = lambda part: pl.BlockSpec((None, None, kb, n2, CH), lambda b, g: (g, part, b, 0, 0))
    tw = pl.BlockSpec((kb, n2, 1), lambda b, g: (b, 0, 0))
    mat = pl.BlockSpec(cst["f2"].shape, lambda b, g: (0, 0))
    in_specs = [slab(0), slab(1), tw, tw, mat]
    args = [a5, a5, cst["tw_r"], cst["tw_i"], cst["f2"]]
    conv = h5 is not None
    if conv:
        hs = lambda part: pl.BlockSpec((None, None, kb, n2, CH), lambda b, g: (0, part, b, 0, 0))
        in_specs += [mat, hs(0), hs(1)]
        args += [cst["f2_inv"], h5, h5]
    return pl.pallas_call(
        functools.partial(_hy_slab_kernel, kb=kb, n2=n2, conv=conv),
        out_shape=jax.ShapeDtypeStruct((groups, 2, n1, n2, CH), jnp.float32),
        grid=(n1 // kb, groups),
        in_specs=in_specs,
        out_specs=pl.BlockSpec((None, 2, kb, n2, CH), lambda b, g: (g, 0, b, 0, 0)),
        compiler_params=_cparams(2),
        name=name,
    )(*args)


def filter_tables(n):
    m = np.arange(2 * n)
    pos = np.where(m < n, m, 2 * n - m)
    pos = np.where(m == n, 0, pos)
    t = pos / (n - 1.0)
    w = 2.0 * np.pi * pos / n
    bands = np.linspace(1e-4, HYENA_BANDS - 1, HYENA_BANDS)
    feat = np.zeros((2 * n, LANE))
    feat[:, 0] = t
    feat[:, 1:1 + HYENA_BANDS] = np.cos(bands[None, :] * w[:, None])
    feat[:, 1 + HYENA_BANDS:HYENA_EMB] = -np.sin(bands[None, :] * w[:, None])
    aux = np.zeros((2 * n, LANE))
    aux[:, 0] = t
    aux[:, 1] = (m < n)
    aux[:, 2] = (m > n)
    deltas = np.abs(np.linspace(HYENA_MIN_DECAY, HYENA_MAX_DECAY, HYENA_WIDTH))[None, :]
    return (jnp.asarray(feat, jnp.float32), jnp.asarray(aux, jnp.float32),
            jnp.asarray(deltas, jnp.float32))


def _filt_kernel(z_ref, aux_ref, dl_ref, w1_ref, b1_ref, w2_ref, b2_ref, w3_ref, b3_ref,
                 w4_ref, b4_ref, fr_ref, o_ref, s_ref):
    hp = lax.Precision.HIGHEST
    f32 = jnp.float32
    fr = fr_ref[...]
    h = jnp.sin(fr[0:1] * (jnp.dot(z_ref[...], w1_ref[...], precision=hp, preferred_element_type=f32)
                           + b1_ref[...]))
    h = jnp.sin(fr[1:2] * (jnp.dot(h, w2_ref[...], precision=hp, preferred_element_type=f32)
                           + b2_ref[...]))
    h = jnp.sin(fr[2:3] * (jnp.dot(h, w3_ref[...], precision=hp, preferred_element_type=f32)
                           + b3_ref[...]))
    f = jnp.dot(h, w4_ref[...], precision=hp, preferred_element_type=f32) + b4_ref[...]
    aux = aux_ref[...]
    dec = jnp.exp(-aux[:, 0:1] * dl_ref[...])
    c = (f[:, :HYENA_WIDTH] * aux[:, 1:2] + f[:, HYENA_WIDTH:] * aux[:, 2:3]) * dec
    o_ref[...] = c

    @pl.when(pl.program_id(0) == 0)
    def _():
        s_ref[...] = jnp.zeros_like(s_ref)

    s_ref[...] += jnp.sum(jnp.abs(c), 0, keepdims=True)


def filter_call(n, tabs, w1p, b1, w2, b2, w3, b3, w4, b4, freq):
    feat, aux, deltas = tabs
    tr = min(2 * n, 1024)
    row = lambda i: (i, 0)
    full = lambda a: pl.BlockSpec(a.shape, lambda i: (0,) * a.ndim)
    ws = [w1p, b1, w2, b2, w3, b3, w4, b4, freq]
    return pl.pallas_call(
        _filt_kernel,
        out_shape=(jax.ShapeDtypeStruct((2 * n, HYENA_WIDTH), jnp.float32),
                   jax.ShapeDtypeStruct((1, HYENA_WIDTH), jnp.float32)),
        grid=(2 * n // tr,),
        in_specs=[pl.BlockSpec((tr, LANE), row), pl.BlockSpec((tr, LANE), row), full(deltas)]
                 + [full(a) for a in ws],
        out_specs=(pl.BlockSpec((tr, HYENA_WIDTH), row),
                   pl.BlockSpec((1, HYENA_WIDTH), lambda i: (0, 0))),
        compiler_params=_cparams(1),
        name="hyena_filter",
    )(feat, aux, deltas, *ws)


def fnet_consts(n, n1):
    cst = dft_consts(n, n1)
    c, s = _cs(FNET_GROUP_DIM, FNET_GROUP_DIM, FNET_GROUP_DIM)
    cst["chan"] = _const_rhs3(np.concatenate([c, s], 0) / math.sqrt(n * FNET_GROUP_DIM))
    return cst


def _fn_slab_kernel(ar_ref, ai_ref, tr_ref, ti_ref, f_ref, cs_ref, o_ref, *, kb, n2):
    f = f_ref[...]
    cs = cs_ref[...]
    gd = FNET_GROUP_DIM
    for j in range(kb):
        ar, ai, tr, ti = ar_ref[j], ai_ref[j], tr_ref[j], ti_ref[j]
        br = ar * tr - ai * ti
        bi = ar * ti + ai * tr
        v = jnp.dot(f, _data_rhs3(jnp.concatenate([br, bi], 0)), preferred_element_type=jnp.float32)
        vr, vi = v[:n2], v[n2:]
        for g in range(FNET_GROUPS):
            lhs = jnp.concatenate([vr[:, g * gd:(g + 1) * gd], vi[:, g * gd:(g + 1) * gd]], 1)
            y = jnp.dot(_data_lhs3(lhs), cs, preferred_element_type=jnp.float32)
            o_ref[:, j * CH + g * gd:j * CH + (g + 1) * gd] = y.astype(o_ref.dtype)


def fn_slab_call(a3, cst, name):
    n1, n2 = cst["n1"], cst["n2"]
    groups = a3.shape[0]
    kb = 8
    a5 = a3.reshape(groups, 2, n1, n2, CH)
    slab = lambda part: pl.BlockSpec((None, None, kb, n2, CH), lambda b, g: (g, part, b, 0, 0))
    tw = pl.BlockSpec((kb, n2, 1), lambda b, g: (b, 0, 0))
    return pl.pallas_call(
        functools.partial(_fn_slab_kernel, kb=kb, n2=n2),
        out_shape=jax.ShapeDtypeStruct((groups, n2, n1 * CH), jnp.bfloat16),
        grid=(n1 // kb, groups),
        in_specs=[slab(0), slab(1), tw, tw,
                  pl.BlockSpec(cst["f2"].shape, lambda b, g: (0, 0)),
                  pl.BlockSpec(cst["chan"].shape, lambda b, g: (0, 0))],
        out_specs=pl.BlockSpec((None, n2, kb * CH), lambda b, g: (g, 0, b)),
        compiler_params=_cparams(2),
        name=name,
    )(a5, a5, cst["tw_r"], cst["tw_i"], cst["f2"], cst["chan"])


def hyena_long(z, x0, n, batch, row_off, cst, tabs, filt_params, d_bias, tag):
    n1, n2 = cst["n1"], cst["n2"]
    w1, b1, w2, b2, w3, b3, w4, b4, freq = filt_params
    w1p = jnp.concatenate([w1, jnp.zeros((LANE - HYENA_EMB, HYENA_FILTER_HIDDEN), w1.dtype)], 0)
    circ, ssum = filter_call(n, tabs, w1p, b1[None, :], w2, b2[None, :], w3, b3[None, :],
                             w4, b4[None, :], freq)
    width = n2 * CH
    h3 = lead_call(cst["lead_real"], circ.reshape(n1, width), n1, 0, 1, "hyena_filt_dft1_" + tag)
    h5 = hy_slab_call(h3, cst, None, "hyena_filt_dft2_" + tag)
    z2d = z.reshape(-1, width)
    x02d = x0.reshape(-1, width)
    pairs = batch // 2
    a3 = lead_call(cst["lead_pair"], z2d, n1, row_off // n2, pairs, "hyena_dft1_" + tag)
    d5 = hy_slab_call(a3, cst, h5, "hyena_conv_" + tag)
    reps = width // CH if width <= 8192 else 8192 // CH
    y = lead_out_call(cst["lead_inv"], d5.reshape(pairs, 2 * n1, width), z2d, x02d,
                      jnp.tile(d_bias[None, :], (1, reps)), jnp.tile(ssum, (1, reps)),
                      n1, row_off // n2, "hyena_idft1_" + tag)
    return y.reshape(batch * n, CH)


def fnet_seq(u, n, batch, row_off, cst, tag):
    n1, n2 = cst["n1"], cst["n2"]
    width = n2 * CH
    a3 = lead_call(cst["lead_real"], u.reshape(-1, width), n1, row_off // n2, batch, "fnet_dft1_" + tag)
    return fn_slab_call(a3, cst, "fnet_dft2_" + tag).reshape(batch * n, CH)


SC_TM = CTX_LEN


def _sconv_kernel(u_ref, p_ref, n_ref, w_ref, b_ref, z_ref, x0_ref, *, lat_tiles, tiles_per_seq):
    i = pl.program_id(0)
    pos = i % tiles_per_seq
    is_lat = i < lat_tiles
    keep_prev = jnp.where(is_lat & (pos != 0), 1.0, 0.0)
    keep_next = jnp.where(is_lat & (pos != tiles_per_seq - 1), 1.0, 0.0)
    u = u_ref[...]
    tm = u.shape[0]
    row = lax.broadcasted_iota(jnp.int32, u.shape, 0)
    um = jnp.where(row == 0, p_ref[7:8, :] * keep_prev, pltpu.roll(u, 1, 0))
    up = jnp.where(row == tm - 1, n_ref[0:1, :] * keep_next, pltpu.roll(u, tm - 1, 0))
    w = w_ref[...]
    uc = b_ref[...] + um * w[0:1] + u * w[1:2] + up * w[2:3]
    x0_ref[...] = uc[:, :CH]
    z_ref[...] = uc[:, CH:2 * CH] * uc[:, 2 * CH:]


def sconv_call(u, conv_w, conv_b, l, n_lat, seq):
    rows = u.shape[0]
    tm = SC_TM
    per8 = tm // 8
    last8 = rows // 8 - 1
    out = jax.ShapeDtypeStruct((rows, CH), jnp.float32)
    return pl.pallas_call(
        functools.partial(_sconv_kernel, lat_tiles=n_lat // tm, tiles_per_seq=seq // tm),
        out_shape=(out, out),
        grid=(rows // tm,),
        in_specs=[pl.BlockSpec((tm, 3 * CH), lambda i: (i, 0)),
                  pl.BlockSpec((8, 3 * CH), lambda i: (jnp.maximum(i * per8 - 1, 0), 0)),
                  pl.BlockSpec((8, 3 * CH), lambda i: (jnp.minimum((i + 1) * per8, last8), 0)),
                  pl.BlockSpec((None, 3, 3 * CH), lambda i: (l, 0, 0)),
                  pl.BlockSpec((None, 1, 3 * CH), lambda i: (l, 0, 0))],
        out_specs=(pl.BlockSpec((tm, CH), lambda i: (i, 0)), pl.BlockSpec((tm, CH), lambda i: (i, 0))),
        compiler_params=_cparams(1),
        name="hyena_short_conv",
    )(u, u, u, conv_w, conv_b)


RANK_TM = 512


def _expert_onehots(r):
    lane = lax.broadcasted_iota(jnp.int32, r.shape, 1).astype(jnp.float32)
    return (jnp.where(lane == r[:, 0:1], 1.0, 0.0), jnp.where(lane == r[:, 1:2], 1.0, 0.0))


def _lanes01(v0, v1, shape):
    lane = lax.broadcasted_iota(jnp.int32, shape, 1)
    return jnp.where(lane == 0, v0, jnp.where(lane == 1, v1, 0.0))


def _rank_kernel(r_ref, o_ref, c_ref, cnt_ref):
    @pl.when(pl.program_id(0) == 0)
    def _():
        cnt_ref[...] = jnp.zeros_like(cnt_ref)

    r = r_ref[...]
    tm = r.shape[0]
    oh0, oh1 = _expert_onehots(r)
    earlier = (lax.broadcasted_iota(jnp.int32, (tm, tm), 1)
               < lax.broadcasted_iota(jnp.int32, (tm, tm), 0))
    tri = jnp.where(earlier, 1.0, 0.0).astype(jnp.bfloat16)
    pre0 = jnp.dot(tri, oh0.astype(jnp.bfloat16), preferred_element_type=jnp.float32)
    pre1 = jnp.dot(tri, oh1.astype(jnp.bfloat16), preferred_element_type=jnp.float32)
    cnt = cnt_ref[...]
    tot0 = jnp.sum(oh0, 0, keepdims=True)
    rank0 = jnp.sum((pre0 + cnt) * oh0, -1, keepdims=True)
    rank1 = jnp.sum((pre1 + cnt + tot0) * oh1, -1, keepdims=True)
    o_ref[...] = _lanes01(rank0, rank1, r.shape)
    cnt_ref[...] = cnt + tot0 + jnp.sum(oh1, 0, keepdims=True)
    c_ref[...] = cnt_ref[...]


def _dest_kernel(r_ref, k_ref, ps_ref, o_ref):
    r = r_ref[...]
    k = k_ref[...]
    oh0, oh1 = _expert_onehots(r)
    ps = ps_ref[...]
    d0 = jnp.sum(oh0 * ps, -1, keepdims=True) + k[:, 0:1]
    d1 = jnp.sum(oh1 * ps, -1, keepdims=True) + k[:, 1:2]
    o_ref[...] = _lanes01(d0, d1, r.shape).astype(jnp.int32)


def _dispatch(route):
    tm = RANK_TM
    row = pl.BlockSpec((tm, ROUTE_LANES), lambda i: (i, 0))
    one = pl.BlockSpec((1, ROUTE_LANES), lambda i: (0, 0))
    rank, counts = pl.pallas_call(
        _rank_kernel,
        out_shape=(jax.ShapeDtypeStruct((NT, ROUTE_LANES), jnp.float32),
                   jax.ShapeDtypeStruct((1, ROUTE_LANES), jnp.float32)),
        grid=(NT // tm,),
        in_specs=[row],
        out_specs=(row, one),
        scratch_shapes=[pltpu.VMEM((1, ROUTE_LANES), jnp.float32)],
        compiler_params=_cparams(1),
        name="moe_rank",
    )(route)
    counts = counts[0].astype(jnp.int32)
    padded = ((counts + MOE_TB - 1) // MOE_TB) * MOE_TB
    lane = jnp.arange(ROUTE_LANES, dtype=jnp.int32)
    pend = jnp.sum(jnp.where(lane[None, :] <= lane[:, None], padded[None, :], 0), 1)
    pstart = (pend - padded).astype(jnp.float32)[None, :]
    dest = pl.pallas_call(
        _dest_kernel,
        out_shape=jax.ShapeDtypeStruct((NT, ROUTE_LANES), jnp.int32),
        grid=(NT // tm,),
        in_specs=[row, row, one],
        out_specs=row,
        compiler_params=_cparams(1),
        name="moe_dest",
    )(route, rank, pstart)
    dest_t = dest[:, :TOP_K].reshape(NT // DISP_TM, DISP_TM, TOP_K).transpose(0, 2, 1)
    pend_e = pend[:N_EXPERTS]
    starts = jnp.arange(MOE_BLOCKS, dtype=jnp.int32) * MOE_TB
    block_e = jnp.minimum(jnp.sum((pend_e[None, :] <= starts[:, None]).astype(jnp.int32), 1),
                          N_EXPERTS - 1)
    n_used = (pend_e[N_EXPERTS - 1] // MOE_TB).astype(jnp.int32).reshape(1)
    return dest_t, block_e, n_used


def kernel(x, c, ctx, c_ctx, w_ada, b_ada, w_in, conv_w, conv_b, hy_w1, hy_b1, hy_w2, hy_b2, hy_w3, hy_b3, hy_w4, hy_b4, hy_freq, hy_dbias, attn_sink, w_br_attn, w_br_hyena, w_br_fnet, w_out, ln1_g, ln1_b, ln2_g, ln2_b, rg_w, rg_b, re_w, re_b, moe_w1, moe_w3, moe_w2):
    bf16 = jnp.bfloat16
    xs = jnp.concatenate([x.reshape(N_LAT, D_MODEL), ctx.reshape(N_CTX, D_MODEL)], 0)
    c8 = jnp.concatenate([c, c_ctx[None, :], jnp.zeros((MOD_ROWS - BATCH - 1, D_MODEL), c.dtype)], 0)
    mod4 = ada_call(c8, w_ada, b_ada).reshape(DEPTH, MOD_ROWS, 1, 6 * D_MODEL)
    cos_t, sin_t = rope_tables(512)
    ln1_g3, ln1_b3 = ln1_g.reshape(DEPTH, 1, D_MODEL), ln1_b.reshape(DEPTH, 1, D_MODEL)
    ln2_g3, ln2_b3 = ln2_g.reshape(DEPTH, 1, D_MODEL), ln2_b.reshape(DEPTH, 1, D_MODEL)
    pad = jnp.zeros((DEPTH, D_MODEL, ROUTE_LANES - N_GROUPS - N_EXPERTS), jnp.float32)
    w_route = jnp.concatenate([rg_w, re_w, pad], -1).astype(bf16)
    b_route = jnp.concatenate([rg_b, re_b, pad[:, 0]], -1).reshape(DEPTH, 1, ROUTE_LANES)

    hy_lat, hy_ctx = dft_consts(2 * SEQ, 64), dft_consts(2 * CTX_LEN, 16)
    fn_lat, fn_ctx = fnet_consts(SEQ, 64), fnet_consts(CTX_LEN, 16)
    tab_lat, tab_ctx = filter_tables(SEQ), filter_tables(CTX_LEN)
    conv_b3 = conv_b.reshape(DEPTH, 1, 3 * HYENA_WIDTH)

    xb = jnp.zeros((MOE_SLOTS, D_MODEL), jnp.float32)
    h = pre_call(xs, mod4)
    for l in range(DEPTH):
        qkv = qkv_call(h, w_in, l, cos_t, sin_t)
        u_hy = proj_call(h, w_in, l, HY_OFF, 3 * HYENA_WIDTH, None, jnp.float32, "proj_hyena")
        u_fn = proj_call(h, w_in, l, FN_OFF, FNET_WIDTH, None, jnp.float32, "proj_fnet")
        gates = proj_call(h, w_in, l, GATE_OFF, 3 * D_MODEL, "sigmoid", bf16, "proj_gates")
        a = attn_call(qkv, attn_sink, l)
        filt = (hy_w1[l], hy_b1[l], hy_w2[l], hy_b2[l], hy_w3[l], hy_b3[l], hy_w4[l], hy_b4[l], hy_freq[l])
        z, x0 = sconv_call(u_hy, conv_w, conv_b3, l, N_LAT, SEQ)
        y_hy = jnp.concatenate(
            [hyena_long(z, x0, SEQ, BATCH, 0, hy_lat, tab_lat, filt, hy_dbias[l], "lat"),
             hyena_long(z, x0, CTX_LEN, BATCH, N_LAT, hy_ctx, tab_ctx, filt, hy_dbias[l], "ctx")], 0)
        y_fn = jnp.concatenate([fnet_seq(u_fn, SEQ, BATCH, 0, fn_lat, "lat"),
                                fnet_seq(u_fn, CTX_LEN, BATCH, N_LAT, fn_ctx, "ctx")], 0)
        xs, t = mix_call(a, y_hy, y_fn, gates, xs,
                         w_br_attn[l].astype(bf16), w_br_hyena[l].astype(bf16),
                         w_br_fnet[l].astype(bf16), w_out[l].astype(bf16),
                         mod4, ln1_g3, ln1_b3, l)
        route = route_call(t, w_route[l], b_route[l])
        dest_t, block_e, n_used = _dispatch(route)
        xb = disp_call(dest_t, t, xb)
        yb = moe_call(block_e, n_used, xb, moe_w1, moe_w3, moe_w2, l)
        if l < DEPTH - 1:
            xs, h = post_call(xs, yb, dest_t, route, mod4, ln2_g3, ln2_b3, l)
    return post_call(xs, yb, dest_t, route, mod4, ln2_g3, ln2_b3, DEPTH - 1).reshape(BATCH, SEQ, D_MODEL)
```

```python
import functools
import math

import numpy as np
import jax
import jax.numpy as jnp
from jax import lax
from jax.experimental import pallas as pl
from jax.experimental.pallas import tpu as pltpu

D_MODEL = 2048
BATCH = 4
SEQ = 4096
DEPTH = 4
GRID_W = 64
CTX_LEN = 256

N_HEADS = 8
N_KV_HEADS = 2
HEAD_DIM = 128
Q_GROUP = N_HEADS // N_KV_HEADS
ATTN_WIDTH = N_HEADS * HEAD_DIM
KV_WIDTH = N_KV_HEADS * HEAD_DIM
WINDOW = 128
ROPE_THETA = 10000.0
MASK_VALUE = -1e30

HYENA_WIDTH = 512
HYENA_SHORT = 3
HYENA_BANDS = 16
HYENA_EMB = 1 + 2 * HYENA_BANDS
HYENA_FILTER_HIDDEN = 64
HYENA_TARGET = 1e-2
HYENA_MAX_DECAY = math.log(HYENA_TARGET) / 0.3
HYENA_MIN_DECAY = math.log(HYENA_TARGET) / 1.5

FNET_WIDTH = 512
FNET_GROUPS = 4
FNET_GROUP_DIM = FNET_WIDTH // FNET_GROUPS

Q_OFF = 0
K_OFF = Q_OFF + ATTN_WIDTH
V_OFF = K_OFF + KV_WIDTH
HY_OFF = V_OFF + KV_WIDTH
FN_OFF = HY_OFF + 3 * HYENA_WIDTH
GATE_OFF = FN_OFF + FNET_WIDTH
IN_WIDTH = GATE_OFF + 3 * D_MODEL
QKV_WIDTH = HY_OFF

N_GROUPS = 4
EXPERTS_PER_GROUP = 8
N_EXPERTS = N_GROUPS * EXPERTS_PER_GROUP
TOP_K = 2
EXPERT_HIDDEN = 512

LN_EPS = 1e-6
DEEPNORM_ALPHA = (2 * DEPTH) ** 0.25

N_LAT = BATCH * SEQ
N_CTX = BATCH * CTX_LEN
NT = N_LAT + N_CTX
MOD_ROWS = 8

LANE = 128
VMEM_LIMIT = 56 * 1024 * 1024

MOE_TB = 256
MOE_BLOCKS = -(-(NT * TOP_K + N_EXPERTS * (MOE_TB - 1)) // MOE_TB)
MOE_SLOTS = MOE_BLOCKS * MOE_TB


def _cparams(n_axes):
    return pltpu.CompilerParams(dimension_semantics=("arbitrary",) * n_axes,
                                vmem_limit_bytes=VMEM_LIMIT)


def _mod_row(i, tm):
    return jnp.minimum((i * tm) // SEQ, BATCH)


def _mod_spec(l, chunk, tm, axis=0):
    def imap(*g):
        return (l, _mod_row(g[axis], tm), 0, chunk)
    return pl.BlockSpec((None, None, 1, D_MODEL), imap)


def _ln(x):
    mu = jnp.mean(x, -1, keepdims=True)
    xc = x - mu
    var = jnp.mean(xc * xc, -1, keepdims=True)
    return xc * lax.rsqrt(var + LN_EPS)


def _ada_kernel(c_ref, w_ref, b_ref, o_ref):
    c = c_ref[...]
    a = (c * jax.nn.sigmoid(c)).astype(jnp.bfloat16)
    o_ref[...] = jnp.dot(a, w_ref[...].astype(jnp.bfloat16),
                         preferred_element_type=jnp.float32) + b_ref[...]


def ada_call(c8, w_ada, b_ada):
    tn = 1024
    n = 6 * D_MODEL
    return pl.pallas_call(
        _ada_kernel,
        out_shape=jax.ShapeDtypeStruct((DEPTH, MOD_ROWS, n), jnp.float32),
        grid=(DEPTH, n // tn),
        in_specs=[pl.BlockSpec((MOD_ROWS, D_MODEL), lambda l, j: (0, 0)),
                  pl.BlockSpec((None, D_MODEL, tn), lambda l, j: (l, 0, j)),
                  pl.BlockSpec((None, 1, tn), lambda l, j: (l, 0, j))],
        out_specs=pl.BlockSpec((None, MOD_ROWS, tn), lambda l, j: (l, 0, j)),
        compiler_params=_cparams(2),
        name="adaln",
    )(c8, w_ada, b_ada.reshape(DEPTH, 1, n))


def _pre_kernel(x_ref, sh_ref, sc_ref, h_ref):
    h_ref[...] = (_ln(x_ref[...]) * (1.0 + sc_ref[...]) + sh_ref[...]).astype(h_ref.dtype)


def pre_call(x, mod4):
    tm = 512
    return pl.pallas_call(
        _pre_kernel,
        out_shape=jax.ShapeDtypeStruct((NT, D_MODEL), jnp.bfloat16),
        grid=(NT // tm,),
        in_specs=[pl.BlockSpec((tm, D_MODEL), lambda i: (i, 0)),
                  _mod_spec(0, 0, tm), _mod_spec(0, 1, tm)],
        out_specs=pl.BlockSpec((tm, D_MODEL), lambda i: (i, 0)),
        compiler_params=_cparams(1),
        name="pre_modulate",
    )(x, mod4, mod4)


def _proj_kernel(x_ref, w_ref, o_ref, wb_ref, *, act):
    @pl.when(pl.program_id(1) == 0)
    def _():
        wb_ref[...] = w_ref[...].astype(jnp.bfloat16)

    acc = jnp.dot(x_ref[...], wb_ref[...], preferred_element_type=jnp.float32)
    if act == "sigmoid":
        acc = jax.nn.sigmoid(acc)
    o_ref[...] = acc.astype(o_ref.dtype)


def proj_call(h, w_in, l, col_off, width, act, out_dtype, name):
    tm, tn = 1024, 512
    off = col_off // tn
    return pl.pallas_call(
        functools.partial(_proj_kernel, act=act),
        out_shape=jax.ShapeDtypeStruct((NT, width), out_dtype),
        grid=(width // tn, NT // tm),
        in_specs=[pl.BlockSpec((tm, D_MODEL), lambda j, i: (i, 0)),
                  pl.BlockSpec((None, D_MODEL, tn), lambda j, i: (l, 0, off + j))],
        out_specs=pl.BlockSpec((tm, tn), lambda j, i: (i, j)),
        scratch_shapes=[pltpu.VMEM((D_MODEL, tn), jnp.bfloat16)],
        compiler_params=_cparams(2),
        name=name,
    )(h, w_in)


def _swap32(x):
    lane = lax.broadcasted_iota(jnp.int32, x.shape, 1)
    up = pltpu.roll(x, LANE - 32, 1)
    dn = pltpu.roll(x, 32, 1)
    return jnp.where((lane & 63) < 32, up, dn)


def _qkv_kernel(x_ref, w_ref, cos_ref, sin_ref, o_ref, wb_ref):
    @pl.when(pl.program_id(0) == 0)
    def _():
        wb_ref[...] = w_ref[...].astype(jnp.bfloat16)

    cos = cos_ref[...]
    sin = sin_ref[...]
    scale = HEAD_DIM ** -0.5
    pair = 2 * HEAD_DIM
    for c in range(QKV_WIDTH // pair):
        acc2 = jnp.dot(x_ref[...], wb_ref[:, c * pair:(c + 1) * pair],
                       preferred_element_type=jnp.float32)
        for half in range(2):
            hd = 2 * c + half
            acc = acc2[:, half * HEAD_DIM:(half + 1) * HEAD_DIM]
            if hd < N_HEADS + N_KV_HEADS:
                acc = acc * cos + _swap32(acc) * sin
            if hd < N_HEADS:
                acc = acc * scale
            o_ref[:, hd * HEAD_DIM:(hd + 1) * HEAD_DIM] = acc.astype(o_ref.dtype)


def qkv_call(h, w_in, l, cos_t, sin_t):
    tm = 512
    n_pos = SEQ // tm

    def rope_map(i):
        return (jnp.where(i * tm < N_LAT, i % n_pos, n_pos), 0)

    return pl.pallas_call(
        _qkv_kernel,
        out_shape=jax.ShapeDtypeStruct((NT, QKV_WIDTH), jnp.bfloat16),
        grid=(NT // tm,),
        in_specs=[pl.BlockSpec((tm, D_MODEL), lambda i: (i, 0)),
                  pl.BlockSpec((None, D_MODEL, QKV_WIDTH), lambda i: (l, 0, 0),
                               pipeline_mode=pl.Buffered(1)),
                  pl.BlockSpec((tm, HEAD_DIM), rope_map),
                  pl.BlockSpec((tm, HEAD_DIM), rope_map)],
        out_specs=pl.BlockSpec((tm, QKV_WIDTH), lambda i: (i, 0)),
        scratch_shapes=[pltpu.VMEM((D_MODEL, QKV_WIDTH), jnp.bfloat16)],
        compiler_params=_cparams(1),
        name="qkv_rope",
    )(h, w_in, cos_t, sin_t)


def rope_tables(tm):
    n_freq = HEAD_DIM // 4
    inv_freq = ROPE_THETA ** (-np.arange(n_freq, dtype=np.float64) / n_freq)
    t = np.arange(SEQ)
    ang_row = (t // GRID_W)[:, None] * inv_freq[None, :]
    ang_col = (t % GRID_W)[:, None] * inv_freq[None, :]
    cos = np.concatenate([np.cos(ang_row), np.cos(ang_row), np.cos(ang_col), np.cos(ang_col)], 1)
    sin = np.concatenate([-np.sin(ang_row), np.sin(ang_row), -np.sin(ang_col), np.sin(ang_col)], 1)
    cos = np.concatenate([cos, np.ones((tm, HEAD_DIM))], 0)
    sin = np.concatenate([sin, np.zeros((tm, HEAD_DIM))], 0)
    return jnp.asarray(cos, jnp.float32), jnp.asarray(sin, jnp.float32)


ATT_QT = 256
ATT_LAT_TILES = N_LAT // ATT_QT
ATT_TILES_PER_SEQ = SEQ // ATT_QT


def _attn_kernel(sink_ref, q_ref, kp_ref, kc_ref, kn_ref, vp_ref, vc_ref, vn_ref,
                 kx_ref, vx_ref, o_ref, *, l):
    i = pl.program_id(0)
    h = pl.program_id(1)
    is_lat = i < ATT_LAT_TILES
    base = (i % ATT_TILES_PER_SEQ) * ATT_QT
    n_loc = ATT_QT + 2 * WINDOW
    r = lax.broadcasted_iota(jnp.int32, (ATT_QT, n_loc), 0)
    j = lax.broadcasted_iota(jnp.int32, (ATT_QT, n_loc), 1)
    kpos = base - WINDOW + j
    dist = j - r
    k_end = jnp.where(is_lat, SEQ, 0)
    valid = (dist >= 0) & (dist <= 2 * WINDOW) & (kpos >= 0) & (kpos < k_end)

    k_loc = jnp.concatenate([kp_ref[...], kc_ref[...], kn_ref[...]], axis=0)
    v_loc = jnp.concatenate([vp_ref[...], vc_ref[...], vn_ref[...]], axis=0)
    k_ctx = kx_ref[...]
    v_ctx = vx_ref[...]
    dn = (((1,), (1,)), ((), ()))
    for g in range(Q_GROUP):
        sl = slice(g * HEAD_DIM, (g + 1) * HEAD_DIM)
        q = q_ref[:, sl]
        s_loc = lax.dot_general(q, k_loc, dn, preferred_element_type=jnp.float32)
        s_loc = jnp.where(valid, s_loc, MASK_VALUE)
        s_ctx = lax.dot_general(q, k_ctx, dn, preferred_element_type=jnp.float32)
        sink = sink_ref[l, h * Q_GROUP + g]
        m = jnp.maximum(jnp.maximum(jnp.max(s_loc, -1, keepdims=True),
                                    jnp.max(s_ctx, -1, keepdims=True)), sink)
        e_loc = jnp.exp(s_loc - m)
        e_ctx = jnp.exp(s_ctx - m)
        den = (jnp.sum(e_loc, -1, keepdims=True) + jnp.sum(e_ctx, -1, keepdims=True)
               + jnp.exp(sink - m))
        o = (jnp.dot(e_loc.astype(jnp.bfloat16), v_loc, preferred_element_type=jnp.float32)
             + jnp.dot(e_ctx.astype(jnp.bfloat16), v_ctx, preferred_element_type=jnp.float32))
        o_ref[:, sl] = (o / den).astype(o_ref.dtype)


def attn_call(qkv, sink, l):
    blk = WINDOW
    per = ATT_QT // blk
    last_blk = NT // blk - 1
    kcol, vcol = K_OFF // HEAD_DIM, V_OFF // HEAD_DIM

    def batch_of(i):
        return jnp.where(i < ATT_LAT_TILES, i // ATT_TILES_PER_SEQ, i - ATT_LAT_TILES)

    def prev_map(col):
        return lambda i, h: (jnp.maximum(i * per - 1, 0), col + h)

    def next_map(col):
        return lambda i, h: (jnp.minimum(i * per + per, last_blk), col + h)

    def cur_map(col):
        return lambda i, h: (i, col + h)

    def ctx_map(col):
        return lambda i, h: (N_LAT // CTX_LEN + batch_of(i), col + h)

    return pl.pallas_call(
        functools.partial(_attn_kernel, l=l),
        out_shape=jax.ShapeDtypeStruct((NT, ATTN_WIDTH), jnp.bfloat16),
        grid=(NT // ATT_QT, N_KV_HEADS),
        in_specs=[pl.BlockSpec(memory_space=pltpu.SMEM),
                  pl.BlockSpec((ATT_QT, Q_GROUP * HEAD_DIM), lambda i, h: (i, h)),
                  pl.BlockSpec((blk, HEAD_DIM), prev_map(kcol)),
                  pl.BlockSpec((ATT_QT, HEAD_DIM), cur_map(kcol)),
                  pl.BlockSpec((blk, HEAD_DIM), next_map(kcol)),
                  pl.BlockSpec((blk, HEAD_DIM), prev_map(vcol)),
                  pl.BlockSpec((ATT_QT, HEAD_DIM), cur_map(vcol)),
                  pl.BlockSpec((blk, HEAD_DIM), next_map(vcol)),
                  pl.BlockSpec((CTX_LEN, HEAD_DIM), ctx_map(kcol)),
                  pl.BlockSpec((CTX_LEN, HEAD_DIM), ctx_map(vcol))],
        out_specs=pl.BlockSpec((ATT_QT, Q_GROUP * HEAD_DIM), lambda i, h: (i, h)),
        compiler_params=_cparams(2),
        name="window_attention",
    )(sink, qkv, qkv, qkv, qkv, qkv, qkv, qkv, qkv, qkv)


def _mix_kernel(a_ref, yh_ref, yf_ref, g_ref, x_ref, wa_ref, wh_ref, wf_ref, wo_ref,
                g1_ref, sh2_ref, sc2_ref, lg_ref, lb_ref, xo_ref, t_ref, y_ref):
    tn = 512
    for c in range(D_MODEL // tn):
        sl = slice(c * tn, (c + 1) * tn)
        ya = jnp.dot(a_ref[...], wa_ref[:, sl], preferred_element_type=jnp.float32)
        yh = jnp.dot(yh_ref[...], wh_ref[:, sl], preferred_element_type=jnp.float32)
        yf = jnp.dot(yf_ref[...], wf_ref[:, sl], preferred_element_type=jnp.float32)
        y = (g_ref[:, sl].astype(jnp.float32) * ya
             + g_ref[:, D_MODEL + c * tn:D_MODEL + (c + 1) * tn].astype(jnp.float32) * yh
             + g_ref[:, 2 * D_MODEL + c * tn:2 * D_MODEL + (c + 1) * tn].astype(jnp.float32) * yf)
        y_ref[:, sl] = y.astype(y_ref.dtype)
    o = jnp.dot(y_ref[...], wo_ref[...], preferred_element_type=jnp.float32)
    xn = _ln(DEEPNORM_ALPHA * x_ref[...] + g1_ref[...] * o) * lg_ref[...] + lb_ref[...]
    xo_ref[...] = xn
    t_ref[...] = (_ln(xn) * (1.0 + sc2_ref[...]) + sh2_ref[...]).astype(t_ref.dtype)


def mix_call(a, yh, yf, gates, x, wa, wh, wf, wo, mod4, ln_g, ln_b, l):
    tm = 256
    row = lambda i: (i, 0)
    full = lambda i: (0, 0)
    const = pl.Buffered(1)
    return pl.pallas_call(
        _mix_kernel,
        out_shape=(jax.ShapeDtypeStruct((NT, D_MODEL), jnp.float32),
                   jax.ShapeDtypeStruct((NT, D_MODEL), jnp.float32)),
        grid=(NT // tm,),
        in_specs=[pl.BlockSpec((tm, ATTN_WIDTH), row),
                  pl.BlockSpec((tm, HYENA_WIDTH), row),
                  pl.BlockSpec((tm, FNET_WIDTH), row),
                  pl.BlockSpec((tm, 3 * D_MODEL), row),
                  pl.BlockSpec((tm, D_MODEL), row),
                  pl.BlockSpec((ATTN_WIDTH, D_MODEL), full, pipeline_mode=const),
                  pl.BlockSpec((HYENA_WIDTH, D_MODEL), full, pipeline_mode=const),
                  pl.BlockSpec((FNET_WIDTH, D_MODEL), full, pipeline_mode=const),
                  pl.BlockSpec((D_MODEL, D_MODEL), full, pipeline_mode=const),
                  _mod_spec(l, 2, tm), _mod_spec(l, 3, tm), _mod_spec(l, 4, tm),
                  pl.BlockSpec((None, 1, D_MODEL), lambda i: (l, 0, 0)),
                  pl.BlockSpec((None, 1, D_MODEL), lambda i: (l, 0, 0))],
        out_specs=(pl.BlockSpec((tm, D_MODEL), row), pl.BlockSpec((tm, D_MODEL), row)),
        scratch_shapes=[pltpu.VMEM((tm, D_MODEL), jnp.bfloat16)],
        compiler_params=_cparams(1),
        name="mixer_out",
    )(a, yh, yf, gates, x, wa, wh, wf, wo, mod4, mod4, mod4, ln_g, ln_b)


ROUTE_LANES = LANE


def _route_kernel(t_ref, w_ref, b_ref, o_ref):
    logits = jnp.dot(t_ref[...].astype(jnp.bfloat16), w_ref[...],
                     preferred_element_type=jnp.float32) + b_ref[...]
    lane = lax.broadcasted_iota(jnp.int32, logits.shape, 1)
    neg = -jnp.inf
    big = jnp.int32(2 ** 30)

    def first_argmax(vals):
        mx = jnp.max(vals, -1, keepdims=True)
        idx = jnp.min(jnp.where(vals == mx, lane, big), -1, keepdims=True)
        return mx, idx

    g_vals = jnp.where(lane < N_GROUPS, logits, neg)
    g_max, grp = first_argmax(g_vals)
    g_w = 1.0 / jnp.sum(jnp.exp(g_vals - g_max), -1, keepdims=True)
    lo = N_GROUPS + grp * EXPERTS_PER_GROUP
    e_vals = jnp.where((lane >= lo) & (lane < lo + EXPERTS_PER_GROUP), logits, neg)
    v1, i1 = first_argmax(e_vals)
    v2, i2 = first_argmax(jnp.where(lane == i1, neg, e_vals))
    e2 = jnp.exp(v2 - v1)
    w1 = g_w / (1.0 + e2)
    w2 = g_w * e2 / (1.0 + e2)
    id1 = (i1 - N_GROUPS).astype(jnp.float32)
    id2 = (i2 - N_GROUPS).astype(jnp.float32)
    o_ref[...] = jnp.where(lane == 0, id1, jnp.where(lane == 1, id2,
                           jnp.where(lane == 2, w1, jnp.where(lane == 3, w2, 0.0))))


def route_call(t, wr, br):
    tm = 512
    return pl.pallas_call(
        _route_kernel,
        out_shape=jax.ShapeDtypeStruct((NT, ROUTE_LANES), jnp.float32),
        grid=(NT // tm,),
        in_specs=[pl.BlockSpec((tm, D_MODEL), lambda i: (i, 0)),
                  pl.BlockSpec((D_MODEL, ROUTE_LANES), lambda i: (0, 0)),
                  pl.BlockSpec((1, ROUTE_LANES), lambda i: (0, 0))],
        out_specs=pl.BlockSpec((tm, ROUTE_LANES), lambda i: (i, 0)),
        compiler_params=_cparams(1),
        name="moe_router",
    )(t, wr, br)


DISP_TM = 256


def _wait_rows(src, dst, sem):
    pltpu.make_async_copy(src, dst, sem).wait()


def _disp_kernel(d_ref, t_hbm, xin_hbm, x_hbm, tbuf, lsem, ssem):
    del xin_hbm
    i = pl.program_id(0)
    n = pl.num_programs(0)

    def load(tile, s):
        return pltpu.make_async_copy(t_hbm.at[pl.ds(tile * DISP_TM, DISP_TM)], tbuf.at[s], lsem.at[s])

    def wait_scatter(s):
        for _k in range(TOP_K):
            _wait_rows(tbuf.at[s], x_hbm.at[pl.ds(0, DISP_TM)], ssem.at[s])

    @pl.when(i == 0)
    def _():
        load(0, 0).start()

    def step(slot):
        load(i, slot).wait()

        @pl.when(i >= 1)
        def _():
            wait_scatter(1 - slot)

        @pl.when(i + 1 < n)
        def _():
            load(i + 1, 1 - slot).start()

        for k in range(TOP_K):
            for r in range(DISP_TM):
                pltpu.make_async_copy(tbuf.at[slot, pl.ds(r, 1)], x_hbm.at[pl.ds(d_ref[k, r], 1)],
                                      ssem.at[slot]).start()

        @pl.when(i == n - 1)
        def _():
            wait_scatter(slot)

    for parity in range(2):
        pl.when(i % 2 == parity)(functools.partial(step, parity))


def disp_call(dest_t, t, xb):
    return pl.pallas_call(
        _disp_kernel,
        out_shape=jax.ShapeDtypeStruct((MOE_SLOTS, D_MODEL), jnp.float32),
        grid=(NT // DISP_TM,),
        in_specs=[pl.BlockSpec((None, TOP_K, DISP_TM), lambda i: (i, 0, 0), memory_space=pltpu.SMEM),
                  pl.BlockSpec(memory_space=pl.ANY),
                  pl.BlockSpec(memory_space=pl.ANY)],
        out_specs=pl.BlockSpec(memory_space=pl.ANY),
        scratch_shapes=[pltpu.VMEM((2, DISP_TM, D_MODEL), jnp.float32),
                        pltpu.SemaphoreType.DMA((2,)),
                        pltpu.SemaphoreType.DMA((2,))],
        input_output_aliases={2: 0},
        compiler_params=_cparams(1),
        name="moe_dispatch",
    )(dest_t, t, xb)


def _moe_kernel(be_ref, nu_ref, x_ref, w1_ref, w3_ref, w2_ref, o_ref, b1_ref, b3_ref, b2_ref):
    i = pl.program_id(0)
    first = (i == 0) | (be_ref[i] != be_ref[jnp.maximum(i - 1, 0)])

    @pl.when(first)
    def _():
        b1_ref[...] = w1_ref[...].astype(jnp.bfloat16)
        b3_ref[...] = w3_ref[...].astype(jnp.bfloat16)
        b2_ref[...] = w2_ref[...].astype(jnp.bfloat16)

    @pl.when(i < nu_ref[0])
    def _():
        x = x_ref[...].astype(jnp.bfloat16)
        h1 = jnp.dot(x, b1_ref[...], preferred_element_type=jnp.float32)
        h3 = jnp.dot(x, b3_ref[...], preferred_element_type=jnp.float32)
        hh = (h1 * jax.nn.sigmoid(h1) * h3).astype(jnp.bfloat16)
        o_ref[...] = jnp.dot(hh, b2_ref[...], preferred_element_type=jnp.float32)

    @pl.when(i >= nu_ref[0])
    def _():
        o_ref[...] = jnp.zeros_like(o_ref)


def moe_call(block_e, n_used, xb, w1, w3, w2, l):
    grid_spec = pltpu.PrefetchScalarGridSpec(
        num_scalar_prefetch=2,
        grid=(MOE_BLOCKS,),
        in_specs=[pl.BlockSpec((MOE_TB, D_MODEL), lambda i, be, nu: (i, 0)),
                  pl.BlockSpec((None, None, D_MODEL, EXPERT_HIDDEN), lambda i, be, nu: (l, be[i], 0, 0)),
                  pl.BlockSpec((None, None, D_MODEL, EXPERT_HIDDEN), lambda i, be, nu: (l, be[i], 0, 0)),
                  pl.BlockSpec((None, None, EXPERT_HIDDEN, D_MODEL), lambda i, be, nu: (l, be[i], 0, 0))],
        out_specs=pl.BlockSpec((MOE_TB, D_MODEL), lambda i, be, nu: (i, 0)),
        scratch_shapes=[pltpu.VMEM((D_MODEL, EXPERT_HIDDEN), jnp.bfloat16),
                        pltpu.VMEM((D_MODEL, EXPERT_HIDDEN), jnp.bfloat16),
                        pltpu.VMEM((EXPERT_HIDDEN, D_MODEL), jnp.bfloat16)])
    return pl.pallas_call(
        _moe_kernel,
        out_shape=jax.ShapeDtypeStruct((MOE_SLOTS, D_MODEL), jnp.float32),
        grid_spec=grid_spec,
        compiler_params=_cparams(1),
        name="moe_experts",
    )(block_e, n_used, xb, w1, w3, w2)


def _post_kernel(d_ref, dn_ref, x_ref, r_ref, g2_ref, lg_ref, lb_ref, *rest, last):
    if last:
        y_hbm, xo_ref, ybuf, sem = rest
    else:
        sh_ref, sc_ref, y_hbm, xo_ref, h_ref, ybuf, sem = rest
    i = pl.program_id(0)
    n = pl.num_programs(0)

    def gather(idx_ref, s):
        for k in range(TOP_K):
            for r in range(DISP_TM):
                pltpu.make_async_copy(y_hbm.at[pl.ds(idx_ref[k, r], 1)], ybuf.at[s, k, pl.ds(r, 1)],
                                      sem.at[s]).start()

    @pl.when(i == 0)
    def _():
        gather(d_ref, 0)

    def step(slot):
        for k in range(TOP_K):
            _wait_rows(y_hbm.at[pl.ds(0, DISP_TM)], ybuf.at[slot, k], sem.at[slot])

        @pl.when(i + 1 < n)
        def _():
            gather(dn_ref, 1 - slot)

        r = r_ref[...]
        m = r[:, 2:3] * ybuf[slot, 0] + r[:, 3:4] * ybuf[slot, 1]
        xn = _ln(DEEPNORM_ALPHA * x_ref[...] + g2_ref[...] * m) * lg_ref[...] + lb_ref[...]
        xo_ref[...] = xn
        if not last:
            h_ref[...] = (_ln(xn) * (1.0 + sc_ref[...]) + sh_ref[...]).astype(h_ref.dtype)

    for parity in range(2):
        pl.when(i % 2 == parity)(functools.partial(step, parity))


def post_call(x, yb, dest_t, route, mod4, ln_g, ln_b, l):
    tm = DISP_TM
    last = l == DEPTH - 1
    rows = N_LAT if last else NT
    n_tiles = rows // tm
    row = lambda i: (i, 0)
    x_spec = pl.BlockSpec((tm, D_MODEL), row)
    smem = lambda f: pl.BlockSpec((None, TOP_K, tm), f, memory_space=pltpu.SMEM)
    in_specs = [smem(lambda i: (i, 0, 0)),
                smem(lambda i: (jnp.minimum(i + 1, n_tiles - 1), 0, 0)),
                x_spec,
                pl.BlockSpec((tm, ROUTE_LANES), row),
                _mod_spec(l, 5, tm),
                pl.BlockSpec((None, 1, D_MODEL), lambda i: (l, 0, 0)),
                pl.BlockSpec((None, 1, D_MODEL), lambda i: (l, 0, 0))]
    args = [dest_t, dest_t, x, route, mod4, ln_g, ln_b]
    x_out = jax.ShapeDtypeStruct((rows, D_MODEL), jnp.float32)
    if last:
        out_shape, out_specs = x_out, x_spec
    else:
        in_specs += [_mod_spec(l + 1, 0, tm), _mod_spec(l + 1, 1, tm)]
        args += [mod4, mod4]
        out_shape = (x_out, jax.ShapeDtypeStruct((rows, D_MODEL), jnp.bfloat16))
        out_specs = (x_spec, x_spec)
    in_specs.append(pl.BlockSpec(memory_space=pl.ANY))
    args.append(yb)
    return pl.pallas_call(
        functools.partial(_post_kernel, last=last),
        out_shape=out_shape,
        grid=(n_tiles,),
        in_specs=in_specs,
        out_specs=out_specs,
        scratch_shapes=[pltpu.VMEM((2, TOP_K, tm, D_MODEL), jnp.float32),
                        pltpu.SemaphoreType.DMA((2,))],
        compiler_params=_cparams(1),
        name="moe_combine_postln",
    )(*args)


CH = HYENA_WIDTH


def _hi_lo(x):
    hi = x.astype(jnp.bfloat16)
    lo = (x - hi.astype(jnp.float32)).astype(jnp.bfloat16)
    return hi, lo


def _const_lhs3(m):
    hi, lo = _hi_lo(jnp.asarray(m, jnp.float32))
    return jnp.concatenate([hi, lo, hi], 1)


def _data_rhs3(x):
    hi, lo = _hi_lo(x)
    return jnp.concatenate([hi, hi, lo], 0)


def _const_rhs3(m):
    hi, lo = _hi_lo(jnp.asarray(m, jnp.float32))
    return jnp.concatenate([hi, hi, lo], 0)


def _data_lhs3(x):
    hi, lo = _hi_lo(x)
    return jnp.concatenate([hi, lo, hi], 1)


def _cs(n, rows, cols):
    ang = 2.0 * np.pi * np.outer(np.arange(rows), np.arange(cols)) / n
    return np.cos(ang), np.sin(ang)


def dft_consts(n_fft, n1):
    n2 = n_fft // n1
    c1, s1 = _cs(n1, n1, n1)
    c2, s2 = _cs(n2, n2, n2)
    h = n1 // 2
    tr, ti = _cs(n_fft, n1, n2)
    return dict(
        n1=n1, n2=n2,
        lead_real=_const_lhs3(np.concatenate([c1, -s1], 0)),
        lead_pair=_const_lhs3(np.block([[c1[:, :h], s1[:, :h]], [-s1[:, :h], c1[:, :h]]])),
        lead_inv=_const_lhs3(np.block([[c1[:h, :], -s1[:h, :]], [s1[:h, :], c1[:h, :]]]) / n_fft),
        f2=_const_lhs3(np.block([[c2, s2], [-s2, c2]])),
        f2_inv=_const_lhs3(np.block([[c2, -s2], [s2, c2]])),
        tw_r=jnp.asarray(tr[:, :, None], jnp.float32),
        tw_i=jnp.asarray(-ti[:, :, None], jnp.float32),
    )


def _lead_kernel(m_ref, x_ref, o_ref):
    o_ref[...] = jnp.dot(m_ref[...], _data_rhs3(x_ref[...]), preferred_element_type=jnp.float32)


def lead_call(mcat, x2d, k, row_off, groups, name):
    mo = mcat.shape[0]
    width = x2d.shape[1]
    tn = min(width, 8192)
    off = row_off // k
    return pl.pallas_call(
        _lead_kernel,
        out_shape=jax.ShapeDtypeStruct((groups, mo, width), jnp.float32),
        grid=(groups, width // tn),
        in_specs=[pl.BlockSpec(mcat.shape, lambda g, j: (0, 0)),
                  pl.BlockSpec((k, tn), lambda g, j: (off + g, j))],
        out_specs=pl.BlockSpec((None, mo, tn), lambda g, j: (g, 0, j)),
        compiler_params=_cparams(2),
        name=name,
    )(mcat, x2d)


def _lead_out_kernel(m_ref, d_ref, z_ref, x0_ref, db_ref, s_ref, o_ref):
    acc = jnp.dot(m_ref[...], _data_rhs3(d_ref[...]), preferred_element_type=jnp.float32)
    y = acc * (1.0 / (s_ref[...] + 1e-6)) + z_ref[...] * db_ref[...]
    o_ref[...] = (x0_ref[...] * y).astype(o_ref.dtype)


def lead_out_call(mcat, d3, z2d, x02d, dbias_t, ssum_t, k_out, row_off, name):
    groups, k_in, width = d3.shape
    tn = min(width, 8192)
    off = row_off // k_out
    view = pl.BlockSpec((k_out, tn), lambda g, j: (off + g, j))
    lanes = pl.BlockSpec((1, tn), lambda g, j: (0, 0))
    return pl.pallas_call(
        _lead_out_kernel,
        out_shape=jax.ShapeDtypeStruct((groups, k_out, width), jnp.bfloat16),
        grid=(groups, width // tn),
        in_specs=[pl.BlockSpec(mcat.shape, lambda g, j: (0, 0)),
                  pl.BlockSpec((None, k_in, tn), lambda g, j: (g, 0, j)),
                  view, view, lanes, lanes],
        out_specs=pl.BlockSpec((None, k_out, tn), lambda g, j: (g, 0, j)),
        compiler_params=_cparams(2),
        name=name,
    )(mcat, d3, z2d, x02d, dbias_t, ssum_t)


def _hy_slab_kernel(*refs, kb, n2, conv):
    if conv:
        ar_ref, ai_ref, tr_ref, ti_ref, f_ref, fi_ref, hr_ref, hi_ref, o_ref = refs
    else:
        ar_ref, ai_ref, tr_ref, ti_ref, f_ref, o_ref = refs
    f = f_ref[...]
    for j in range(kb):
        ar, ai, tr, ti = ar_ref[j], ai_ref[j], tr_ref[j], ti_ref[j]
        br = ar * tr - ai * ti
        bi = ar * ti + ai * tr
        x = jnp.dot(f, _data_rhs3(jnp.concatenate([br, bi], 0)), preferred_element_type=jnp.float32)
        xr, xi = x[:n2], x[n2:]
        if conv:
            hr, hi = hr_ref[j], hi_ref[j]
            yr = xr * hr - xi * hi
            yi = xr * hi + xi * hr
            d = jnp.dot(fi_ref[...], _data_rhs3(jnp.concatenate([yr, yi], 0)),
                        preferred_element_type=jnp.float32)
            dr, di = d[:n2], d[n2:]
            o_ref[0, j] = dr * tr + di * ti
            o_ref[1, j] = di * tr - dr * ti
        else:
            o_ref[0, j] = xr
            o_ref[1, j] = xi


def hy_slab_call(a3, cst, h5, name):
    n1, n2 = cst["n1"], cst["n2"]
    groups = a3.shape[0]
    kb = 8
    a5 = a3.reshape(groups, 2, n1, n2, CH)
    slab = lambda part: pl.BlockSpec((None, None, kb, n2, CH), lambda b, g: (g, part, b, 0, 0))
    tw = pl.BlockSpec((kb, n2, 1), lambda b, g: (b, 0, 0))
    mat = pl.BlockSpec(cst["f2"].shape, lambda b, g: (0, 0))
    in_specs = [slab(0), slab(1), tw, tw, mat]
    args = [a5, a5, cst["tw_r"], cst["tw_i"], cst["f2"]]
    conv = h5 is not None
    if conv:
        hs = lambda part: pl.BlockSpec((None, None, kb, n2, CH), lambda b, g: (0, part, b, 0, 0))
        in_specs += [mat, hs(0), hs(1)]
        args += [cst["f2_inv"], h5, h5]
    return pl.pallas_call(
        functools.partial(_hy_slab_kernel, kb=kb, n2=n2, conv=conv),
        out_shape=jax.ShapeDtypeStruct((groups, 2, n1, n2, CH), jnp.float32),
        grid=(n1 // kb, groups),
        in_specs=in_specs,
        out_specs=pl.BlockSpec((None, 2, kb, n2, CH), lambda b, g: (g, 0, b, 0, 0)),
        compiler_params=_cparams(2),
        name=name,
    )(*args)


def filter_tables(n):
    m = np.arange(2 * n)
    pos = np.where(m < n, m, 2 * n - m)
    pos = np.where(m == n, 0, pos)
    t = pos / (n - 1.0)
    w = 2.0 * np.pi * pos / n
    bands = np.linspace(1e-4, HYENA_BANDS - 1, HYENA_BANDS)
    feat = np.zeros((2 * n, LANE))
    feat[:, 0] = t
    feat[:, 1:1 + HYENA_BANDS] = np.cos(bands[None, :] * w[:, None])
    feat[:, 1 + HYENA_BANDS:HYENA_EMB] = -np.sin(bands[None, :] * w[:, None])
    aux = np.zeros((2 * n, LANE))
    aux[:, 0] = t
    aux[:, 1] = (m < n)
    aux[:, 2] = (m > n)
    deltas = np.abs(np.linspace(HYENA_MIN_DECAY, HYENA_MAX_DECAY, HYENA_WIDTH))[None, :]
    return (jnp.asarray(feat, jnp.float32), jnp.asarray(aux, jnp.float32),
            jnp.asarray(deltas, jnp.float32))


def _filt_kernel(z_ref, aux_ref, dl_ref, w1_ref, b1_ref, w2_ref, b2_ref, w3_ref, b3_ref,
                 w4_ref, b4_ref, fr_ref, o_ref, s_ref):
    hp = lax.Precision.HIGHEST
    f32 = jnp.float32
    fr = fr_ref[...]
    h = jnp.sin(fr[0:1] * (jnp.dot(z_ref[...], w1_ref[...], precision=hp, preferred_element_type=f32)
                           + b1_ref[...]))
    h = jnp.sin(fr[1:2] * (jnp.dot(h, w2_ref[...], precision=hp, preferred_element_type=f32)
                           + b2_ref[...]))
    h = jnp.sin(fr[2:3] * (jnp.dot(h, w3_ref[...], precision=hp, preferred_element_type=f32)
                           + b3_ref[...]))
    f = jnp.dot(h, w4_ref[...], precision=hp, preferred_element_type=f32) + b4_ref[...]
    aux = aux_ref[...]
    dec = jnp.exp(-aux[:, 0:1] * dl_ref[...])
    c = (f[:, :HYENA_WIDTH] * aux[:, 1:2] + f[:, HYENA_WIDTH:] * aux[:, 2:3]) * dec
    o_ref[...] = c

    @pl.when(pl.program_id(0) == 0)
    def _():
        s_ref[...] = jnp.zeros_like(s_ref)

    s_ref[...] += jnp.sum(jnp.abs(c), 0, keepdims=True)


def filter_call(n, tabs, w1p, b1, w2, b2, w3, b3, w4, b4, freq):
    feat, aux, deltas = tabs
    tr = min(2 * n, 1024)
    row = lambda i: (i, 0)
    full = lambda a: pl.BlockSpec(a.shape, lambda i: (0,) * a.ndim)
    ws = [w1p, b1, w2, b2, w3, b3, w4, b4, freq]
    return pl.pallas_call(
        _filt_kernel,
        out_shape=(jax.ShapeDtypeStruct((2 * n, HYENA_WIDTH), jnp.float32),
                   jax.ShapeDtypeStruct((1, HYENA_WIDTH), jnp.float32)),
        grid=(2 * n // tr,),
        in_specs=[pl.BlockSpec((tr, LANE), row), pl.BlockSpec((tr, LANE), row), full(deltas)]
                 + [full(a) for a in ws],
        out_specs=(pl.BlockSpec((tr, HYENA_WIDTH), row),
                   pl.BlockSpec((1, HYENA_WIDTH), lambda i: (0, 0))),
        compiler_params=_cparams(1),
        name="hyena_filter",
    )(feat, aux, deltas, *ws)


def fnet_consts(n, n1):
    cst = dft_consts(n, n1)
    c, s = _cs(FNET_GROUP_DIM, FNET_GROUP_DIM, FNET_GROUP_DIM)
    cst["chan"] = _const_rhs3(np.concatenate([c, s], 0) / math.sqrt(n * FNET_GROUP_DIM))
    return cst


def _fn_slab_kernel(ar_ref, ai_ref, tr_ref, ti_ref, f_ref, cs_ref, o_ref, *, kb, n2):
    f = f_ref[...]
    cs = cs_ref[...]
    gd = FNET_GROUP_DIM
    for j in range(kb):
        ar, ai, tr, ti = ar_ref[j], ai_ref[j], tr_ref[j], ti_ref[j]
        br = ar * tr - ai * ti
        bi = ar * ti + ai * tr
        v = jnp.dot(f, _data_rhs3(jnp.concatenate([br, bi], 0)), preferred_element_type=jnp.float32)
        vr, vi = v[:n2], v[n2:]
        for g in range(FNET_GROUPS):
            lhs = jnp.concatenate([vr[:, g * gd:(g + 1) * gd], vi[:, g * gd:(g + 1) * gd]], 1)
            y = jnp.dot(_data_lhs3(lhs), cs, preferred_element_type=jnp.float32)
            o_ref[:, j * CH + g * gd:j * CH + (g + 1) * gd] = y.astype(o_ref.dtype)


def fn_slab_call(a3, cst, name):
    n1, n2 = cst["n1"], cst["n2"]
    groups = a3.shape[0]
    kb = 8
    a5 = a3.reshape(groups, 2, n1, n2, CH)
    slab = lambda part: pl.BlockSpec((None, None, kb, n2, CH), lambda b, g: (g, part, b, 0, 0))
    tw = pl.BlockSpec((kb, n2, 1), lambda b, g: (b, 0, 0))
    return pl.pallas_call(
        functools.partial(_fn_slab_kernel, kb=kb, n2=n2),
        out_shape=jax.ShapeDtypeStruct((groups, n2, n1 * CH), jnp.bfloat16),
        grid=(n1 // kb, groups),
        in_specs=[slab(0), slab(1), tw, tw,
                  pl.BlockSpec(cst["f2"].shape, lambda b, g: (0, 0)),
                  pl.BlockSpec(cst["chan"].shape, lambda b, g: (0, 0))],
        out_specs=pl.BlockSpec((None, n2, kb * CH), lambda b, g: (g, 0, b)),
        compiler_params=_cparams(2),
        name=name,
    )(a5, a5, cst["tw_r"], cst["tw_i"], cst["f2"], cst["chan"])


def hyena_long(z, x0, n, batch, row_off, cst, tabs, filt_params, d_bias, tag):
    n1, n2 = cst["n1"], cst["n2"]
    w1, b1, w2, b2, w3, b3, w4, b4, freq = filt_params
    w1p = jnp.concatenate([w1, jnp.zeros((LANE - HYENA_EMB, HYENA_FILTER_HIDDEN), w1.dtype)], 0)
    circ, ssum = filter_call(n, tabs, w1p, b1[None, :], w2, b2[None, :], w3, b3[None, :],
                             w4, b4[None, :], freq)
    width = n2 * CH
    h3 = lead_call(cst["lead_real"], circ.reshape(n1, width), n1, 0, 1, "hyena_filt_dft1_" + tag)
    h5 = hy_slab_call(h3, cst, None, "hyena_filt_dft2_" + tag)
    z2d = z.reshape(-1, width)
    x02d = x0.reshape(-1, width)
    pairs = batch // 2
    a3 = lead_call(cst["lead_pair"], z2d, n1, row_off // n2, pairs, "hyena_dft1_" + tag)
    d5 = hy_slab_call(a3, cst, h5, "hyena_conv_" + tag)
    reps = width // CH if width <= 8192 else 8192 // CH
    y = lead_out_call(cst["lead_inv"], d5.reshape(pairs, 2 * n1, width), z2d, x02d,
                      jnp.tile(d_bias[None, :], (1, reps)), jnp.tile(ssum, (1, reps)),
                      n1, row_off // n2, "hyena_idft1_" + tag)
    return y.reshape(batch * n, CH)


def fnet_seq(u, n, batch, row_off, cst, tag):
    n1, n2 = cst["n1"], cst["n2"]
    width = n2 * CH
    a3 = lead_call(cst["lead_real"], u.reshape(-1, width), n1, row_off // n2, batch, "fnet_dft1_" + tag)
    return fn_slab_call(a3, cst, "fnet_dft2_" + tag).reshape(batch * n, CH)


SC_TM = CTX_LEN


def _sconv_kernel(u_ref, p_ref, n_ref, w_ref, b_ref, z_ref, x0_ref, *, lat_tiles, tiles_per_seq):
    i = pl.program_id(0)
    pos = i % tiles_per_seq
    is_lat = i < lat_tiles
    keep_prev = jnp.where(is_lat & (pos != 0), 1.0, 0.0)
    keep_next = jnp.where(is_lat & (pos != tiles_per_seq - 1), 1.0, 0.0)
    u = u_ref[...]
    tm = u.shape[0]
    row = lax.broadcasted_iota(jnp.int32, u.shape, 0)
    um = jnp.where(row == 0, p_ref[7:8, :] * keep_prev, pltpu.roll(u, 1, 0))
    up = jnp.where(row == tm - 1, n_ref[0:1, :] * keep_next, pltpu.roll(u, tm - 1, 0))
    w = w_ref[...]
    uc = b_ref[...] + um * w[0:1] + u * w[1:2] + up * w[2:3]
    x0_ref[...] = uc[:, :CH]
    z_ref[...] = uc[:, CH:2 * CH] * uc[:, 2 * CH:]


def sconv_call(u, conv_w, conv_b, l, n_lat, seq):
    rows = u.shape[0]
    tm = SC_TM
    per8 = tm // 8
    last8 = rows // 8 - 1
    out = jax.ShapeDtypeStruct((rows, CH), jnp.float32)
    return pl.pallas_call(
        functools.partial(_sconv_kernel, lat_tiles=n_lat // tm, tiles_per_seq=seq // tm),
        out_shape=(out, out),
        grid=(rows // tm,),
        in_specs=[pl.BlockSpec((tm, 3 * CH), lambda i: (i, 0)),
                  pl.BlockSpec((8, 3 * CH), lambda i: (jnp.maximum(i * per8 - 1, 0), 0)),
                  pl.BlockSpec((8, 3 * CH), lambda i: (jnp.minimum((i + 1) * per8, last8), 0)),
                  pl.BlockSpec((None, 3, 3 * CH), lambda i: (l, 0, 0)),
                  pl.BlockSpec((None, 1, 3 * CH), lambda i: (l, 0, 0))],
        out_specs=(pl.BlockSpec((tm, CH), lambda i: (i, 0)), pl.BlockSpec((tm, CH), lambda i: (i, 0))),
        compiler_params=_cparams(1),
        name="hyena_short_conv",
    )(u, u, u, conv_w, conv_b)


RANK_TM = 512


def _expert_onehots(r):
    lane = lax.broadcasted_iota(jnp.int32, r.shape, 1).astype(jnp.float32)
    return (jnp.where(lane == r[:, 0:1], 1.0, 0.0), jnp.where(lane == r[:, 1:2], 1.0, 0.0))


def _lanes01(v0, v1, shape):
    lane = lax.broadcasted_iota(jnp.int32, shape, 1)
    return jnp.where(lane == 0, v0, jnp.where(lane == 1, v1, 0.0))


def _rank_kernel(r_ref, o_ref, c_ref, cnt_ref):
    @pl.when(pl.program_id(0) == 0)
    def _():
        cnt_ref[...] = jnp.zeros_like(cnt_ref)

    r = r_ref[...]
    tm = r.shape[0]
    oh0, oh1 = _expert_onehots(r)
    earlier = (lax.broadcasted_iota(jnp.int32, (tm, tm), 1)
               < lax.broadcasted_iota(jnp.int32, (tm, tm), 0))
    tri = jnp.where(earlier, 1.0, 0.0).astype(jnp.bfloat16)
    pre0 = jnp.dot(tri, oh0.astype(jnp.bfloat16), preferred_element_type=jnp.float32)
    pre1 = jnp.dot(tri, oh1.astype(jnp.bfloat16), preferred_element_type=jnp.float32)
    cnt = cnt_ref[...]
    tot0 = jnp.sum(oh0, 0, keepdims=True)
    rank0 = jnp.sum((pre0 + cnt) * oh0, -1, keepdims=True)
    rank1 = jnp.sum((pre1 + cnt + tot0) * oh1, -1, keepdims=True)
    o_ref[...] = _lanes01(rank0, rank1, r.shape)
    cnt_ref[...] = cnt + tot0 + jnp.sum(oh1, 0, keepdims=True)
    c_ref[...] = cnt_ref[...]


def _dest_kernel(r_ref, k_ref, ps_ref, o_ref):
    r = r_ref[...]
    k = k_ref[...]
    oh0, oh1 = _expert_onehots(r)
    ps = ps_ref[...]
    d0 = jnp.sum(oh0 * ps, -1, keepdims=True) + k[:, 0:1]
    d1 = jnp.sum(oh1 * ps, -1, keepdims=True) + k[:, 1:2]
    o_ref[...] = _lanes01(d0, d1, r.shape).astype(jnp.int32)


def _dispatch(route):
    tm = RANK_TM
    row = pl.BlockSpec((tm, ROUTE_LANES), lambda i: (i, 0))
    one = pl.BlockSpec((1, ROUTE_LANES), lambda i: (0, 0))
    rank, counts = pl.pallas_call(
        _rank_kernel,
        out_shape=(jax.ShapeDtypeStruct((NT, ROUTE_LANES), jnp.float32),
                   jax.ShapeDtypeStruct((1, ROUTE_LANES), jnp.float32)),
        grid=(NT // tm,),
        in_specs=[row],
        out_specs=(row, one),
        scratch_shapes=[pltpu.VMEM((1, ROUTE_LANES), jnp.float32)],
        compiler_params=_cparams(1),
        name="moe_rank",
    )(route)
    counts = counts[0].astype(jnp.int32)
    padded = ((counts + MOE_TB - 1) // MOE_TB) * MOE_TB
    lane = jnp.arange(ROUTE_LANES, dtype=jnp.int32)
    pend = jnp.sum(jnp.where(lane[None, :] <= lane[:, None], padded[None, :], 0), 1)
    pstart = (pend - padded).astype(jnp.float32)[None, :]
    dest = pl.pallas_call(
        _dest_kernel,
        out_shape=jax.ShapeDtypeStruct((NT, ROUTE_LANES), jnp.int32),
        grid=(NT // tm,),
        in_specs=[row, row, one],
        out_specs=row,
        compiler_params=_cparams(1),
        name="moe_dest",
    )(route, rank, pstart)
    dest_t = dest[:, :TOP_K].reshape(NT // DISP_TM, DISP_TM, TOP_K).transpose(0, 2, 1)
    pend_e = pend[:N_EXPERTS]
    starts = jnp.arange(MOE_BLOCKS, dtype=jnp.int32) * MOE_TB
    block_e = jnp.minimum(jnp.sum((pend_e[None, :] <= starts[:, None]).astype(jnp.int32), 1),
                          N_EXPERTS - 1)
    n_used = (pend_e[N_EXPERTS - 1] // MOE_TB).astype(jnp.int32).reshape(1)
    return dest_t, block_e, n_used


def kernel(x, c, ctx, c_ctx, w_ada, b_ada, w_in, conv_w, conv_b, hy_w1, hy_b1, hy_w2, hy_b2, hy_w3, hy_b3, hy_w4, hy_b4, hy_freq, hy_dbias, attn_sink, w_br_attn, w_br_hyena, w_br_fnet, w_out, ln1_g, ln1_b, ln2_g, ln2_b, rg_w, rg_b, re_w, re_b, moe_w1, moe_w3, moe_w2):
    bf16 = jnp.bfloat16
    xs = jnp.concatenate([x.reshape(N_LAT, D_MODEL), ctx.reshape(N_CTX, D_MODEL)], 0)
    c8 = jnp.concatenate([c, c_ctx[None, :], jnp.zeros((MOD_ROWS - BATCH - 1, D_MODEL), c.dtype)], 0)
    mod4 = ada_call(c8, w_ada, b_ada).reshape(DEPTH, MOD_ROWS, 1, 6 * D_MODEL)
    cos_t, sin_t = rope_tables(512)
    ln1_g3, ln1_b3 = ln1_g.reshape(DEPTH, 1, D_MODEL), ln1_b.reshape(DEPTH, 1, D_MODEL)
    ln2_g3, ln2_b3 = ln2_g.reshape(DEPTH, 1, D_MODEL), ln2_b.reshape(DEPTH, 1, D_MODEL)
    pad = jnp.zeros((DEPTH, D_MODEL, ROUTE_LANES - N_GROUPS - N_EXPERTS), jnp.float32)
    w_route = jnp.concatenate([rg_w, re_w, pad], -1).astype(bf16)
    b_route = jnp.concatenate([rg_b, re_b, pad[:, 0]], -1).reshape(DEPTH, 1, ROUTE_LANES)

    hy_lat, hy_ctx = dft_consts(2 * SEQ, 64), dft_consts(2 * CTX_LEN, 16)
    fn_lat, fn_ctx = fnet_consts(SEQ, 64), fnet_consts(CTX_LEN, 16)
    tab_lat, tab_ctx = filter_tables(SEQ), filter_tables(CTX_LEN)
    conv_b3 = conv_b.reshape(DEPTH, 1, 3 * HYENA_WIDTH)

    xb = jnp.zeros((MOE_SLOTS, D_MODEL), jnp.float32)
    h = pre_call(xs, mod4)
    for l in range(DEPTH):
        qkv = qkv_call(h, w_in, l, cos_t, sin_t)
        u_hy = proj_call(h, w_in, l, HY_OFF, 3 * HYENA_WIDTH, None, jnp.float32, "proj_hyena")
        u_fn = proj_call(h, w_in, l, FN_OFF, FNET_WIDTH, None, jnp.float32, "proj_fnet")
        gates = proj_call(h, w_in, l, GATE_OFF, 3 * D_MODEL, "sigmoid", bf16, "proj_gates")
        a = attn_call(qkv, attn_sink, l)
        filt = (hy_w1[l], hy_b1[l], hy_w2[l], hy_b2[l], hy_w3[l], hy_b3[l], hy_w4[l], hy_b4[l], hy_freq[l])
        z, x0 = sconv_call(u_hy, conv_w, conv_b3, l, N_LAT, SEQ)
        y_hy = jnp.concatenate(
            [hyena_long(z, x0, SEQ, BATCH, 0, hy_lat, tab_lat, filt, hy_dbias[l], "lat"),
             hyena_long(z, x0, CTX_LEN, BATCH, N_LAT, hy_ctx, tab_ctx, filt, hy_dbias[l], "ctx")], 0)
        y_fn = jnp.concatenate([fnet_seq(u_fn, SEQ, BATCH, 0, fn_lat, "lat"),
                                fnet_seq(u_fn, CTX_LEN, BATCH, N_LAT, fn_ctx, "ctx")], 0)
        xs, t = mix_call(a, y_hy, y_fn, gates, xs,
                         w_br_attn[l].astype(bf16), w_br_hyena[l].astype(bf16),
                         w_br_fnet[l].astype(bf16), w_out[l].astype(bf16),
                         mod4, ln1_g3, ln1_b3, l)
        route = route_call(t, w_route[l], b_route[l])
        dest_t, block_e, n_used = _dispatch(route)
        xb = disp_call(dest_t, t, xb)
        yb = moe_call(block_e, n_used, xb, moe_w1, moe_w3, moe_w2, l)
        if l < DEPTH - 1:
            xs, h = post_call(xs, yb, dest_t, route, mod4, ln2_g3, ln2_b3, l)
    return post_call(xs, yb, dest_t, route, mod4, ln2_g3, ln2_b3, DEPTH - 1).reshape(BATCH, SEQ, D_MODEL)
```

```python
import functools
import math

import numpy as np
import jax
import jax.numpy as jnp
from jax import lax
from jax.experimental import pallas as pl
from jax.experimental.pallas import tpu as pltpu

D_MODEL = 2048
BATCH = 4
SEQ = 4096
DEPTH = 4
GRID_W = 64
CTX_LEN = 256

N_HEADS = 8
N_KV_HEADS = 2
HEAD_DIM = 128
Q_GROUP = N_HEADS // N_KV_HEADS
ATTN_WIDTH = N_HEADS * HEAD_DIM
KV_WIDTH = N_KV_HEADS * HEAD_DIM
WINDOW = 128
ROPE_THETA = 10000.0
MASK_VALUE = -1e30

HYENA_WIDTH = 512
HYENA_SHORT = 3
HYENA_BANDS = 16
HYENA_EMB = 1 + 2 * HYENA_BANDS
HYENA_FILTER_HIDDEN = 64
HYENA_TARGET = 1e-2
HYENA_MAX_DECAY = math.log(HYENA_TARGET) / 0.3
HYENA_MIN_DECAY = math.log(HYENA_TARGET) / 1.5

FNET_WIDTH = 512
FNET_GROUPS = 4
FNET_GROUP_DIM = FNET_WIDTH // FNET_GROUPS

Q_OFF = 0
K_OFF = Q_OFF + ATTN_WIDTH
V_OFF = K_OFF + KV_WIDTH
HY_OFF = V_OFF + KV_WIDTH
FN_OFF = HY_OFF + 3 * HYENA_WIDTH
GATE_OFF = FN_OFF + FNET_WIDTH
IN_WIDTH = GATE_OFF + 3 * D_MODEL
QKV_WIDTH = HY_OFF

N_GROUPS = 4
EXPERTS_PER_GROUP = 8
N_EXPERTS = N_GROUPS * EXPERTS_PER_GROUP
TOP_K = 2
EXPERT_HIDDEN = 512

LN_EPS = 1e-6
DEEPNORM_ALPHA = (2 * DEPTH) ** 0.25

N_LAT = BATCH * SEQ
N_CTX = BATCH * CTX_LEN
NT = N_LAT + N_CTX
MOD_ROWS = 8

LANE = 128
VMEM_LIMIT = 56 * 1024 * 1024

MOE_TB = 256
MOE_BLOCKS = -(-(NT * TOP_K + N_EXPERTS * (MOE_TB - 1)) // MOE_TB)
MOE_SLOTS = MOE_BLOCKS * MOE_TB


def _cparams(n_axes):
    return pltpu.CompilerParams(dimension_semantics=("arbitrary",) * n_axes,
                                vmem_limit_bytes=VMEM_LIMIT)


def _mod_row(i, tm):
    return jnp.minimum((i * tm) // SEQ, BATCH)


def _mod_spec(l, chunk, tm, axis=0):
    def imap(*g):
        return (l, _mod_row(g[axis], tm), 0, chunk)
    return pl.BlockSpec((None, None, 1, D_MODEL), imap)


def _ln(x):
    mu = jnp.mean(x, -1, keepdims=True)
    xc = x - mu
    var = jnp.mean(xc * xc, -1, keepdims=True)
    return xc * lax.rsqrt(var + LN_EPS)


def _ada_kernel(c_ref, w_ref, b_ref, o_ref):
    c = c_ref[...]
    a = (c * jax.nn.sigmoid(c)).astype(jnp.bfloat16)
    o_ref[...] = jnp.dot(a, w_ref[...].astype(jnp.bfloat16),
                         preferred_element_type=jnp.float32) + b_ref[...]


def ada_call(c8, w_ada, b_ada):
    tn = 1024
    n = 6 * D_MODEL
    return pl.pallas_call(
        _ada_kernel,
        out_shape=jax.ShapeDtypeStruct((DEPTH, MOD_ROWS, n), jnp.float32),
        grid=(DEPTH, n // tn),
        in_specs=[pl.BlockSpec((MOD_ROWS, D_MODEL), lambda l, j: (0, 0)),
                  pl.BlockSpec((None, D_MODEL, tn), lambda l, j: (l, 0, j)),
                  pl.BlockSpec((None, 1, tn), lambda l, j: (l, 0, j))],
        out_specs=pl.BlockSpec((None, MOD_ROWS, tn), lambda l, j: (l, 0, j)),
        compiler_params=_cparams(2),
        name="adaln",
    )(c8, w_ada, b_ada.reshape(DEPTH, 1, n))


def _pre_kernel(x_ref, sh_ref, sc_ref, h_ref):
    h_ref[...] = (_ln(x_ref[...]) * (1.0 + sc_ref[...]) + sh_ref[...]).astype(h_ref.dtype)


def pre_call(x, mod4):
    tm = 512
    return pl.pallas_call(
        _pre_kernel,
        out_shape=jax.ShapeDtypeStruct((NT, D_MODEL), jnp.bfloat16),
        grid=(NT // tm,),
        in_specs=[pl.BlockSpec((tm, D_MODEL), lambda i: (i, 0)),
                  _mod_spec(0, 0, tm), _mod_spec(0, 1, tm)],
        out_specs=pl.BlockSpec((tm, D_MODEL), lambda i: (i, 0)),
        compiler_params=_cparams(1),
        name="pre_modulate",
    )(x, mod4, mod4)


def _proj_kernel(x_ref, w_ref, o_ref, wb_ref, *, act):
    @pl.when(pl.program_id(1) == 0)
    def _():
        wb_ref[...] = w_ref[...].astype(jnp.bfloat16)

    acc = jnp.dot(x_ref[...], wb_ref[...], preferred_element_type=jnp.float32)
    if act == "sigmoid":
        acc = jax.nn.sigmoid(acc)
    o_ref[...] = acc.astype(o_ref.dtype)


def proj_call(h, w_in, l, col_off, width, act, out_dtype, name):
    tm, tn = 1024, 512
    off = col_off // tn
    return pl.pallas_call(
        functools.partial(_proj_kernel, act=act),
        out_shape=jax.ShapeDtypeStruct((NT, width), out_dtype),
        grid=(width // tn, NT // tm),
        in_specs=[pl.BlockSpec((tm, D_MODEL), lambda j, i: (i, 0)),
                  pl.BlockSpec((None, D_MODEL, tn), lambda j, i: (l, 0, off + j))],
        out_specs=pl.BlockSpec((tm, tn), lambda j, i: (i, j)),
        scratch_shapes=[pltpu.VMEM((D_MODEL, tn), jnp.bfloat16)],
        compiler_params=_cparams(2),
        name=name,
    )(h, w_in)


def _swap32(x):
    lane = lax.broadcasted_iota(jnp.int32, x.shape, 1)
    up = pltpu.roll(x, LANE - 32, 1)
    dn = pltpu.roll(x, 32, 1)
    return jnp.where((lane & 63) < 32, up, dn)


def _qkv_kernel(x_ref, w_ref, cos_ref, sin_ref, o_ref, wb_ref):
    @pl.when(pl.program_id(0) == 0)
    def _():
        wb_ref[...] = w_ref[...].astype(jnp.bfloat16)

    cos = cos_ref[...]
    sin = sin_ref[...]
    scale = HEAD_DIM ** -0.5
    pair = 2 * HEAD_DIM
    for c in range(QKV_WIDTH // pair):
        acc2 = jnp.dot(x_ref[...], wb_ref[:, c * pair:(c + 1) * pair],
                       preferred_element_type=jnp.float32)
        for half in range(2):
            hd = 2 * c + half
            acc = acc2[:, half * HEAD_DIM:(half + 1) * HEAD_DIM]
            if hd < N_HEADS + N_KV_HEADS:
                acc = acc * cos + _swap32(acc) * sin
            if hd < N_HEADS:
                acc = acc * scale
            o_ref[:, hd * HEAD_DIM:(hd + 1) * HEAD_DIM] = acc.astype(o_ref.dtype)


def qkv_call(h, w_in, l, cos_t, sin_t):
    tm = 512
    n_pos = SEQ // tm

    def rope_map(i):
        return (jnp.where(i * tm < N_LAT, i % n_pos, n_pos), 0)

    return pl.pallas_call(
        _qkv_kernel,
        out_shape=jax.ShapeDtypeStruct((NT, QKV_WIDTH), jnp.bfloat16),
        grid=(NT // tm,),
        in_specs=[pl.BlockSpec((tm, D_MODEL), lambda i: (i, 0)),
                  pl.BlockSpec((None, D_MODEL, QKV_WIDTH), lambda i: (l, 0, 0),
                               pipeline_mode=pl.Buffered(1)),
                  pl.BlockSpec((tm, HEAD_DIM), rope_map),
                  pl.BlockSpec((tm, HEAD_DIM), rope_map)],
        out_specs=pl.BlockSpec((tm, QKV_WIDTH), lambda i: (i, 0)),
        scratch_shapes=[pltpu.VMEM((D_MODEL, QKV_WIDTH), jnp.bfloat16)],
        compiler_params=_cparams(1),
        name="qkv_rope",
    )(h, w_in, cos_t, sin_t)


def rope_tables(tm):
    n_freq = HEAD_DIM // 4
    inv_freq = ROPE_THETA ** (-np.arange(n_freq, dtype=np.float64) / n_freq)
    t = np.arange(SEQ)
    ang_row = (t // GRID_W)[:, None] * inv_freq[None, :]
    ang_col = (t % GRID_W)[:, None] * inv_freq[None, :]
    cos = np.concatenate([np.cos(ang_row), np.cos(ang_row), np.cos(ang_col), np.cos(ang_col)], 1)
    sin = np.concatenate([-np.sin(ang_row), np.sin(ang_row), -np.sin(ang_col), np.sin(ang_col)], 1)
    cos = np.concatenate([cos, np.ones((tm, HEAD_DIM))], 0)
    sin = np.concatenate([sin, np.zeros((tm, HEAD_DIM))], 0)
    return jnp.asarray(cos, jnp.float32), jnp.asarray(sin, jnp.float32)


ATT_QT = 256
ATT_LAT_TILES = N_LAT // ATT_QT
ATT_TILES_PER_SEQ = SEQ // ATT_QT


def _attn_kernel(sink_ref, q_ref, kp_ref, kc_ref, kn_ref, vp_ref, vc_ref, vn_ref,
                 kx_ref, vx_ref, o_ref, *, l):
    i = pl.program_id(0)
    h = pl.program_id(1)
    is_lat = i < ATT_LAT_TILES
    base = (i % ATT_TILES_PER_SEQ) * ATT_QT
    n_loc = ATT_QT + 2 * WINDOW
    r = lax.broadcasted_iota(jnp.int32, (ATT_QT, n_loc), 0)
    j = lax.broadcasted_iota(jnp.int32, (ATT_QT, n_loc), 1)
    kpos = base - WINDOW + j
    dist = j - r
    k_end = jnp.where(is_lat, SEQ, 0)
    valid = (dist >= 0) & (dist <= 2 * WINDOW) & (kpos >= 0) & (kpos < k_end)

    k_loc = jnp.concatenate([kp_ref[...], kc_ref[...], kn_ref[...]], axis=0)
    v_loc = jnp.concatenate([vp_ref[...], vc_ref[...], vn_ref[...]], axis=0)
    k_ctx = kx_ref[...]
    v_ctx = vx_ref[...]
    dn = (((1,), (1,)), ((), ()))
    for g in range(Q_GROUP):
        sl = slice(g * HEAD_DIM, (g + 1) * HEAD_DIM)
        q = q_ref[:, sl]
        s_loc = lax.dot_general(q, k_loc, dn, preferred_element_type=jnp.float32)
        s_loc = jnp.where(valid, s_loc, MASK_VALUE)
        s_ctx = lax.dot_general(q, k_ctx, dn, preferred_element_type=jnp.float32)
        sink = sink_ref[l, h * Q_GROUP + g]
        m = jnp.maximum(jnp.maximum(jnp.max(s_loc, -1, keepdims=True),
                                    jnp.max(s_ctx, -1, keepdims=True)), sink)
        e_loc = jnp.exp(s_loc - m)
        e_ctx = jnp.exp(s_ctx - m)
        den = (jnp.sum(e_loc, -1, keepdims=True) + jnp.sum(e_ctx, -1, keepdims=True)
               + jnp.exp(sink - m))
        o = (jnp.dot(e_loc.astype(jnp.bfloat16), v_loc, preferred_element_type=jnp.float32)
             + jnp.dot(e_ctx.astype(jnp.bfloat16), v_ctx, preferred_element_type=jnp.float32))
        o_ref[:, sl] = (o / den).astype(o_ref.dtype)


def attn_call(qkv, sink, l):
    blk = WINDOW
    per = ATT_QT // blk
    last_blk = NT // blk - 1
    kcol, vcol = K_OFF // HEAD_DIM, V_OFF // HEAD_DIM

    def batch_of(i):
        return jnp.where(i < ATT_LAT_TILES, i // ATT_TILES_PER_SEQ, i - ATT_LAT_TILES)

    def prev_map(col):
        return lambda i, h: (jnp.maximum(i * per - 1, 0), col + h)

    def next_map(col):
        return lambda i, h: (jnp.minimum(i * per + per, last_blk), col + h)

    def cur_map(col):
        return lambda i, h: (i, col + h)

    def ctx_map(col):
        return lambda i, h: (N_LAT // CTX_LEN + batch_of(i), col + h)

    return pl.pallas_call(
        functools.partial(_attn_kernel, l=l),
        out_shape=jax.ShapeDtypeStruct((NT, ATTN_WIDTH), jnp.bfloat16),
        grid=(NT // ATT_QT, N_KV_HEADS),
        in_specs=[pl.BlockSpec(memory_space=pltpu.SMEM),
                  pl.BlockSpec((ATT_QT, Q_GROUP * HEAD_DIM), lambda i, h: (i, h)),
                  pl.BlockSpec((blk, HEAD_DIM), prev_map(kcol)),
                  pl.BlockSpec((ATT_QT, HEAD_DIM), cur_map(kcol)),
                  pl.BlockSpec((blk, HEAD_DIM), next_map(kcol)),
                  pl.BlockSpec((blk, HEAD_DIM), prev_map(vcol)),
                  pl.BlockSpec((ATT_QT, HEAD_DIM), cur_map(vcol)),
                  pl.BlockSpec((blk, HEAD_DIM), next_map(vcol)),
                  pl.BlockSpec((CTX_LEN, HEAD_DIM), ctx_map(kcol)),
                  pl.BlockSpec((CTX_LEN, HEAD_DIM), ctx_map(vcol))],
        out_specs=pl.BlockSpec((ATT_QT, Q_GROUP * HEAD_DIM), lambda i, h: (i, h)),
        compiler_params=_cparams(2),
        name="window_attention",
    )(sink, qkv, qkv, qkv, qkv, qkv, qkv, qkv, qkv, qkv)


MIX_TM = 256


def _mix_kernel(a_ref, yhl_ref, yhc_ref, yfl_ref, yfc_ref, g_ref, x_ref, wa_ref, wh_ref, wf_ref,
                wo_ref, g1_ref, sh2_ref, sc2_ref, lg_ref, lb_ref, xo_ref, t_ref, y_ref):
    tn = 512
    tile = lax.broadcasted_iota(jnp.int32, yhl_ref.shape, 0) * 0 + pl.program_id(0)
    is_lat = tile < N_LAT // MIX_TM
    yh_in = jnp.where(is_lat, yhl_ref[...], yhc_ref[...])
    yf_in = jnp.where(is_lat, yfl_ref[...], yfc_ref[...])
    for c in range(D_MODEL // tn):
        sl = slice(c * tn, (c + 1) * tn)
        ya = jnp.dot(a_ref[...], wa_ref[:, sl], preferred_element_type=jnp.float32)
        yh = jnp.dot(yh_in, wh_ref[:, sl], preferred_element_type=jnp.float32)
        yf = jnp.dot(yf_in, wf_ref[:, sl], preferred_element_type=jnp.float32)
        y = (g_ref[:, sl].astype(jnp.float32) * ya
             + g_ref[:, D_MODEL + c * tn:D_MODEL + (c + 1) * tn].astype(jnp.float32) * yh
             + g_ref[:, 2 * D_MODEL + c * tn:2 * D_MODEL + (c + 1) * tn].astype(jnp.float32) * yf)
        y_ref[:, sl] = y.astype(y_ref.dtype)
    o = jnp.dot(y_ref[...], wo_ref[...], preferred_element_type=jnp.float32)
    xn = _ln(DEEPNORM_ALPHA * x_ref[...] + g1_ref[...] * o) * lg_ref[...] + lb_ref[...]
    xo_ref[...] = xn
    t_ref[...] = (_ln(xn) * (1.0 + sc2_ref[...]) + sh2_ref[...]).astype(t_ref.dtype)


def mix_call(a, yh_lat, yh_ctx, yf_lat, yf_ctx, gates, x, wa, wh, wf, wo, mod4, ln_g, ln_b, l):
    tm = MIX_TM
    lat_tiles = N_LAT // tm
    row = lambda i: (i, 0)
    lat = lambda i: (jnp.minimum(i, lat_tiles - 1), 0)
    ctx = lambda i: (jnp.maximum(i - lat_tiles, 0), 0)
    full = lambda i: (0, 0)
    const = pl.Buffered(1)
    return pl.pallas_call(
        _mix_kernel,
        out_shape=(jax.ShapeDtypeStruct((NT, D_MODEL), jnp.float32),
                   jax.ShapeDtypeStruct((NT, D_MODEL), jnp.float32)),
        grid=(NT // tm,),
        in_specs=[pl.BlockSpec((tm, ATTN_WIDTH), row),
                  pl.BlockSpec((tm, HYENA_WIDTH), lat),
                  pl.BlockSpec((tm, HYENA_WIDTH), ctx),
                  pl.BlockSpec((tm, FNET_WIDTH), lat),
                  pl.BlockSpec((tm, FNET_WIDTH), ctx),
                  pl.BlockSpec((tm, 3 * D_MODEL), row),
                  pl.BlockSpec((tm, D_MODEL), row),
                  pl.BlockSpec((ATTN_WIDTH, D_MODEL), full, pipeline_mode=const),
                  pl.BlockSpec((HYENA_WIDTH, D_MODEL), full, pipeline_mode=const),
                  pl.BlockSpec((FNET_WIDTH, D_MODEL), full, pipeline_mode=const),
                  pl.BlockSpec((D_MODEL, D_MODEL), full, pipeline_mode=const),
                  _mod_spec(l, 2, tm), _mod_spec(l, 3, tm), _mod_spec(l, 4, tm),
                  pl.BlockSpec((None, 1, D_MODEL), lambda i: (l, 0, 0)),
                  pl.BlockSpec((None, 1, D_MODEL), lambda i: (l, 0, 0))],
        out_specs=(pl.BlockSpec((tm, D_MODEL), row), pl.BlockSpec((tm, D_MODEL), row)),
        scratch_shapes=[pltpu.VMEM((tm, D_MODEL), jnp.bfloat16)],
        compiler_params=_cparams(1),
        name="mixer_out",
    )(a, yh_lat, yh_ctx, yf_lat, yf_ctx, gates, x, wa, wh, wf, wo, mod4, mod4, mod4, ln_g, ln_b)


ROUTE_LANES = LANE


def _route_kernel(t_ref, w_ref, b_ref, o_ref):
    logits = jnp.dot(t_ref[...].astype(jnp.bfloat16), w_ref[...],
                     preferred_element_type=jnp.float32) + b_ref[...]
    lane = lax.broadcasted_iota(jnp.int32, logits.shape, 1)
    neg = -jnp.inf
    big = jnp.int32(2 ** 30)

    def first_argmax(vals):
        mx = jnp.max(vals, -1, keepdims=True)
        idx = jnp.min(jnp.where(vals == mx, lane, big), -1, keepdims=True)
        return mx, idx

    g_vals = jnp.where(lane < N_GROUPS, logits, neg)
    g_max, grp = first_argmax(g_vals)
    g_w = 1.0 / jnp.sum(jnp.exp(g_vals - g_max), -1, keepdims=True)
    lo = N_GROUPS + grp * EXPERTS_PER_GROUP
    e_vals = jnp.where((lane >= lo) & (lane < lo + EXPERTS_PER_GROUP), logits, neg)
    v1, i1 = first_argmax(e_vals)
    v2, i2 = first_argmax(jnp.where(lane == i1, neg, e_vals))
    e2 = jnp.exp(v2 - v1)
    w1 = g_w / (1.0 + e2)
    w2 = g_w * e2 / (1.0 + e2)
    id1 = (i1 - N_GROUPS).astype(jnp.float32)
    id2 = (i2 - N_GROUPS).astype(jnp.float32)
    o_ref[...] = jnp.where(lane == 0, id1, jnp.where(lane == 1, id2,
                           jnp.where(lane == 2, w1, jnp.where(lane == 3, w2, 0.0))))


def route_call(t, wr, br):
    tm = 512
    return pl.pallas_call(
        _route_kernel,
        out_shape=jax.ShapeDtypeStruct((NT, ROUTE_LANES), jnp.float32),
        grid=(NT // tm,),
        in_specs=[pl.BlockSpec((tm, D_MODEL), lambda i: (i, 0)),
                  pl.BlockSpec((D_MODEL, ROUTE_LANES), lambda i: (0, 0)),
                  pl.BlockSpec((1, ROUTE_LANES), lambda i: (0, 0))],
        out_specs=pl.BlockSpec((tm, ROUTE_LANES), lambda i: (i, 0)),
        compiler_params=_cparams(1),
        name="moe_router",
    )(t, wr, br)


DISP_TM = 256


def _wait_rows(src, dst, sem):
    pltpu.make_async_copy(src, dst, sem).wait()


def _disp_kernel(d_ref, t_hbm, xin_hbm, x_hbm, tbuf, lsem, ssem):
    del xin_hbm
    i = pl.program_id(0)
    n = pl.num_programs(0)

    def load(tile, s):
        return pltpu.make_async_copy(t_hbm.at[pl.ds(tile * DISP_TM, DISP_TM)], tbuf.at[s], lsem.at[s])

    def wait_scatter(s):
        for _k in range(TOP_K):
            _wait_rows(tbuf.at[s], x_hbm.at[pl.ds(0, DISP_TM)], ssem.at[s])

    @pl.when(i == 0)
    def _():
        load(0, 0).start()

    def step(slot):
        load(i, slot).wait()

        @pl.when(i >= 1)
        def _():
            wait_scatter(1 - slot)

        @pl.when(i + 1 < n)
        def _():
            load(i + 1, 1 - slot).start()

        for k in range(TOP_K):
            for r in range(DISP_TM):
                pltpu.make_async_copy(tbuf.at[slot, pl.ds(r, 1)], x_hbm.at[pl.ds(d_ref[k, r], 1)],
                                      ssem.at[slot]).start()

        @pl.when(i == n - 1)
        def _():
            wait_scatter(slot)

    for parity in range(2):
        pl.when(i % 2 == parity)(functools.partial(step, parity))


def disp_call(dest_t, t, xb):
    return pl.pallas_call(
        _disp_kernel,
        out_shape=jax.ShapeDtypeStruct((MOE_SLOTS, D_MODEL), jnp.float32),
        grid=(NT // DISP_TM,),
        in_specs=[pl.BlockSpec((None, TOP_K, DISP_TM), lambda i: (i, 0, 0), memory_space=pltpu.SMEM),
                  pl.BlockSpec(memory_space=pl.ANY),
                  pl.BlockSpec(memory_space=pl.ANY)],
        out_specs=pl.BlockSpec(memory_space=pl.ANY),
        scratch_shapes=[pltpu.VMEM((2, DISP_TM, D_MODEL), jnp.float32),
                        pltpu.SemaphoreType.DMA((2,)),
                        pltpu.SemaphoreType.DMA((2,))],
        input_output_aliases={2: 0},
        compiler_params=_cparams(1),
        name="moe_dispatch",
    )(dest_t, t, xb)


def _moe_kernel(be_ref, nu_ref, x_ref, w1_ref, w3_ref, w2_ref, o_ref, b1_ref, b3_ref, b2_ref):
    i = pl.program_id(0)
    first = (i == 0) | (be_ref[i] != be_ref[jnp.maximum(i - 1, 0)])

    @pl.when(first)
    def _():
        b1_ref[...] = w1_ref[...].astype(jnp.bfloat16)
        b3_ref[...] = w3_ref[...].astype(jnp.bfloat16)
        b2_ref[...] = w2_ref[...].astype(jnp.bfloat16)

    @pl.when(i < nu_ref[0])
    def _():
        x = x_ref[...].astype(jnp.bfloat16)
        h1 = jnp.dot(x, b1_ref[...], preferred_element_type=jnp.float32)
        h3 = jnp.dot(x, b3_ref[...], preferred_element_type=jnp.float32)
        hh = (h1 * jax.nn.sigmoid(h1) * h3).astype(jnp.bfloat16)
        o_ref[...] = jnp.dot(hh, b2_ref[...], preferred_element_type=jnp.float32)

    @pl.when(i >= nu_ref[0])
    def _():
        o_ref[...] = jnp.zeros_like(o_ref)


def moe_call(block_e, n_used, xb, w1, w3, w2, l):
    grid_spec = pltpu.PrefetchScalarGridSpec(
        num_scalar_prefetch=2,
        grid=(MOE_BLOCKS,),
        in_specs=[pl.BlockSpec((MOE_TB, D_MODEL), lambda i, be, nu: (i, 0)),
                  pl.BlockSpec((None, None, D_MODEL, EXPERT_HIDDEN), lambda i, be, nu: (l, be[i], 0, 0)),
                  pl.BlockSpec((None, None, D_MODEL, EXPERT_HIDDEN), lambda i, be, nu: (l, be[i], 0, 0)),
                  pl.BlockSpec((None, None, EXPERT_HIDDEN, D_MODEL), lambda i, be, nu: (l, be[i], 0, 0))],
        out_specs=pl.BlockSpec((MOE_TB, D_MODEL), lambda i, be, nu: (i, 0)),
        scratch_shapes=[pltpu.VMEM((D_MODEL, EXPERT_HIDDEN), jnp.bfloat16),
                        pltpu.VMEM((D_MODEL, EXPERT_HIDDEN), jnp.bfloat16),
                        pltpu.VMEM((EXPERT_HIDDEN, D_MODEL), jnp.bfloat16)])
    return pl.pallas_call(
        _moe_kernel,
        out_shape=jax.ShapeDtypeStruct((MOE_SLOTS, D_MODEL), jnp.float32),
        grid_spec=grid_spec,
        compiler_params=_cparams(1),
        name="moe_experts",
    )(block_e, n_used, xb, w1, w3, w2)


def _post_kernel(d_ref, dn_ref, x_ref, r_ref, g2_ref, lg_ref, lb_ref, *rest, last):
    if last:
        y_hbm, xo_ref, ybuf, sem = rest
    else:
        sh_ref, sc_ref, y_hbm, xo_ref, h_ref, ybuf, sem = rest
    i = pl.program_id(0)
    n = pl.num_programs(0)

    def gather(idx_ref, s):
        for k in range(TOP_K):
            for r in range(DISP_TM):
                pltpu.make_async_copy(y_hbm.at[pl.ds(idx_ref[k, r], 1)], ybuf.at[s, k, pl.ds(r, 1)],
                                      sem.at[s]).start()

    @pl.when(i == 0)
    def _():
        gather(d_ref, 0)

    def step(slot):
        for k in range(TOP_K):
            _wait_rows(y_hbm.at[pl.ds(0, DISP_TM)], ybuf.at[slot, k], sem.at[slot])

        @pl.when(i + 1 < n)
        def _():
            gather(dn_ref, 1 - slot)

        r = r_ref[...]
        m = r[:, 2:3] * ybuf[slot, 0] + r[:, 3:4] * ybuf[slot, 1]
        xn = _ln(DEEPNORM_ALPHA * x_ref[...] + g2_ref[...] * m) * lg_ref[...] + lb_ref[...]
        xo_ref[...] = xn
        if not last:
            h_ref[...] = (_ln(xn) * (1.0 + sc_ref[...]) + sh_ref[...]).astype(h_ref.dtype)

    for parity in range(2):
        pl.when(i % 2 == parity)(functools.partial(step, parity))


def post_call(x, yb, dest_t, route, mod4, ln_g, ln_b, l):
    tm = DISP_TM
    last = l == DEPTH - 1
    rows = N_LAT if last else NT
    n_tiles = rows // tm
    row = lambda i: (i, 0)
    x_spec = pl.BlockSpec((tm, D_MODEL), row)
    smem = lambda f: pl.BlockSpec((None, TOP_K, tm), f, memory_space=pltpu.SMEM)
    in_specs = [smem(lambda i: (i, 0, 0)),
                smem(lambda i: (jnp.minimum(i + 1, n_tiles - 1), 0, 0)),
                x_spec,
                pl.BlockSpec((tm, ROUTE_LANES), row),
                _mod_spec(l, 5, tm),
                pl.BlockSpec((None, 1, D_MODEL), lambda i: (l, 0, 0)),
                pl.BlockSpec((None, 1, D_MODEL), lambda i: (l, 0, 0))]
    args = [dest_t, dest_t, x, route, mod4, ln_g, ln_b]
    x_out = jax.ShapeDtypeStruct((rows, D_MODEL), jnp.float32)
    if last:
        out_shape, out_specs = x_out, x_spec
    else:
        in_specs += [_mod_spec(l + 1, 0, tm), _mod_spec(l + 1, 1, tm)]
        args += [mod4, mod4]
        out_shape = (x_out, jax.ShapeDtypeStruct((rows, D_MODEL), jnp.bfloat16))
        out_specs = (x_spec, x_spec)
    in_specs.append(pl.BlockSpec(memory_space=pl.ANY))
    args.append(yb)
    return pl.pallas_call(
        functools.partial(_post_kernel, last=last),
        out_shape=out_shape,
        grid=(n_tiles,),
        in_specs=in_specs,
        out_specs=out_specs,
        scratch_shapes=[pltpu.VMEM((2, TOP_K, tm, D_MODEL), jnp.float32),
                        pltpu.SemaphoreType.DMA((2,))],
        compiler_params=_cparams(1),
        name="moe_combine_postln",
    )(*args)


CH = HYENA_WIDTH


def _hi_lo(x):
    hi = x.astype(jnp.bfloat16)
    lo = (x - hi.astype(jnp.float32)).astype(jnp.bfloat16)
    return hi, lo


def _const_lhs3(m):
    hi, lo = _hi_lo(jnp.asarray(m, jnp.float32))
    return jnp.concatenate([hi, lo, hi], 1)


def _data_rhs3(x):
    hi, lo = _hi_lo(x)
    return jnp.concatenate([hi, hi, lo], 0)


def _const_rhs3(m):
    hi, lo = _hi_lo(jnp.asarray(m, jnp.float32))
    return jnp.concatenate([hi, hi, lo], 0)


def _data_lhs3(x):
    hi, lo = _hi_lo(x)
    return jnp.concatenate([hi, lo, hi], 1)


def _cs(n, rows, cols):
    ang = 2.0 * np.pi * np.outer(np.arange(rows), np.arange(cols)) / n
    return np.cos(ang), np.sin(ang)


def dft_consts(n_fft, n1):
    n2 = n_fft // n1
    c1, s1 = _cs(n1, n1, n1)
    c2, s2 = _cs(n2, n2, n2)
    h = n1 // 2
    tr, ti = _cs(n_fft, n1, n2)
    return dict(
        n1=n1, n2=n2,
        lead_real=_const_lhs3(np.concatenate([c1, -s1], 0)),
        lead_pair=_const_lhs3(np.block([[c1[:, :h], s1[:, :h]], [-s1[:, :h], c1[:, :h]]])),
        lead_inv=_const_lhs3(np.block([[c1[:h, :], -s1[:h, :]], [s1[:h, :], c1[:h, :]]]) / n_fft),
        f2=_const_lhs3(np.block([[c2, s2], [-s2, c2]])),
        f2_inv=_const_lhs3(np.block([[c2, -s2], [s2, c2]])),
        tw_r=jnp.asarray(tr[:, :, None], jnp.float32),
        tw_i=jnp.asarray(-ti[:, :, None], jnp.float32),
    )


def _lead_kernel(m_ref, x_ref, o_ref):
    o_ref[...] = jnp.dot(m_ref[...], _data_rhs3(x_ref[...]), preferred_element_type=jnp.float32)


def lead_call(mcat, x2d, k, row_off, groups, name):
    mo = mcat.shape[0]
    width = x2d.shape[1]
    tn = min(width, 8192)
    off = row_off // k
    return pl.pallas_call(
        _lead_kernel,
        out_shape=jax.ShapeDtypeStruct((groups, mo, width), jnp.float32),
        grid=(groups, width // tn),
        in_specs=[pl.BlockSpec(mcat.shape, lambda g, j: (0, 0)),
                  pl.BlockSpec((k, tn), lambda g, j: (off + g, j))],
        out_specs=pl.BlockSpec((None, mo, tn), lambda g, j: (g, 0, j)),
        compiler_params=_cparams(2),
        name=name,
    )(mcat, x2d)


def _lead_out_kernel(m_ref, d_ref, z_ref, x0_ref, db_ref, s_ref, o_ref):
    acc = jnp.dot(m_ref[...], _data_rhs3(d_ref[...]), preferred_element_type=jnp.float32)
    y = acc * (1.0 / (s_ref[...] + 1e-6)) + z_ref[...] * db_ref[...]
    o_ref[...] = (x0_ref[...] * y).astype(o_ref.dtype)


def lead_out_call(mcat, d3, z2d, x02d, dbias_t, ssum_t, k_out, row_off, name):
    groups, k_in, width = d3.shape
    tn = min(width, 8192)
    off = row_off // k_out
    view = pl.BlockSpec((k_out, tn), lambda g, j: (off + g, j))
    lanes = pl.BlockSpec((1, tn), lambda g, j: (0, 0))
    return pl.pallas_call(
        _lead_out_kernel,
        out_shape=jax.ShapeDtypeStruct((groups, k_out, width), jnp.bfloat16),
        grid=(groups, width // tn),
        in_specs=[pl.BlockSpec(mcat.shape, lambda g, j: (0, 0)),
                  pl.BlockSpec((None, k_in, tn), lambda g, j: (g, 0, j)),
                  view, view, lanes, lanes],
        out_specs=pl.BlockSpec((None, k_out, tn), lambda g, j: (g, 0, j)),
        compiler_params=_cparams(2),
        name=name,
    )(mcat, d3, z2d, x02d, dbias_t, ssum_t)


def _hy_slab_kernel(*refs, kb, n2, conv):
    if conv:
        ar_ref, ai_ref, tr_ref, ti_ref, f_ref, fi_ref, hr_ref, hi_ref, o_ref = refs
    else:
        ar_ref, ai_ref, tr_ref, ti_ref, f_ref, o_ref = refs
    f = f_ref[...]
    for j in range(kb):
        ar, ai, tr, ti = ar_ref[j], ai_ref[j], tr_ref[j], ti_ref[j]
        br = ar * tr - ai * ti
        bi = ar * ti + ai * tr
        x = jnp.dot(f, _data_rhs3(jnp.concatenate([br, bi], 0)), preferred_element_type=jnp.float32)
        xr, xi = x[:n2], x[n2:]
        if conv:
            hr, hi = hr_ref[j], hi_ref[j]
            yr = xr * hr - xi * hi
            yi = xr * hi + xi * hr
            d = jnp.dot(fi_ref[...], _data_rhs3(jnp.concatenate([yr, yi], 0)),
                        preferred_element_type=jnp.float32)
            dr, di = d[:n2], d[n2:]
            o_ref[0, j] = dr * tr + di * ti
            o_ref[1, j] = di * tr - dr * ti
        else:
            o_ref[0, j] = xr
            o_ref[1, j] = xi


def hy_slab_call(a3, cst, h5, name):
    n1, n2 = cst["n1"], cst["n2"]
    groups = a3.shape[0]
    kb = 8
    a5 = a3.reshape(groups, 2, n1, n2, CH)
    slab = lambda part: pl.BlockSpec((None, None, kb, n2, CH), lambda b, g: (g, part, b, 0, 0))
    tw = pl.BlockSpec((kb, n2, 1), lambda b, g: (b, 0, 0))
    mat = pl.BlockSpec(cst["f2"].shape, lambda b, g: (0, 0))
    in_specs = [slab(0), slab(1), tw, tw, mat]
    args = [a5, a5, cst["tw_r"], cst["tw_i"], cst["f2"]]
    conv = h5 is not None
    if conv:
        hs = lambda part: pl.BlockSpec((None, None, kb, n2, CH), lambda b, g: (0, part, b, 0, 0))
        in_specs += [mat, hs(0), hs(1)]
        args += [cst["f2_inv"], h5, h5]
    return pl.pallas_call(
        functools.partial(_hy_slab_kernel, kb=kb, n2=n2, conv=conv),
        out_shape=jax.ShapeDtypeStruct((groups, 2, n1, n2, CH), jnp.float32),
        grid=(n1 // kb, groups),
        in_specs=in_specs,
        out_specs=pl.BlockSpec((None, 2, kb, n2, CH), lambda b, g: (g, 0, b, 0, 0)),
        compiler_params=_cparams(2),
        name=name,
    )(*args)


def filter_tables(n):
    m = np.arange(2 * n)
    pos = np.where(m < n, m, 2 * n - m)
    pos = np.where(m == n, 0, pos)
    t = pos / (n - 1.0)
    w = 2.0 * np.pi * pos / n
    bands = np.linspace(1e-4, HYENA_BANDS - 1, HYENA_BANDS)
    feat = np.zeros((2 * n, LANE))
    feat[:, 0] = t
    feat[:, 1:1 + HYENA_BANDS] = np.cos(bands[None, :] * w[:, None])
    feat[:, 1 + HYENA_BANDS:HYENA_EMB] = -np.sin(bands[None, :] * w[:, None])
    aux = np.zeros((2 * n, LANE))
    aux[:, 0] = t
    aux[:, 1] = (m < n)
    aux[:, 2] = (m > n)
    deltas = np.abs(np.linspace(HYENA_MIN_DECAY, HYENA_MAX_DECAY, HYENA_WIDTH))[None, :]
    return (jnp.asarray(feat, jnp.float32), jnp.asarray(aux, jnp.float32),
            jnp.asarray(deltas, jnp.float32))


def _filt_kernel(z_ref, aux_ref, dl_ref, w1_ref, b1_ref, w2_ref, b2_ref, w3_ref, b3_ref,
                 w4_ref, b4_ref, fr_ref, o_ref, s_ref):
    hp = lax.Precision.HIGHEST
    f32 = jnp.float32
    fr = fr_ref[...]
    h = jnp.sin(fr[0:1] * (jnp.dot(z_ref[...], w1_ref[...], precision=hp, preferred_element_type=f32)
                           + b1_ref[...]))
    h = jnp.sin(fr[1:2] * (jnp.dot(h, w2_ref[...], precision=hp, preferred_element_type=f32)
                           + b2_ref[...]))
    h = jnp.sin(fr[2:3] * (jnp.dot(h, w3_ref[...], precision=hp, preferred_element_type=f32)
                           + b3_ref[...]))
    f = jnp.dot(h, w4_ref[...], precision=hp, preferred_element_type=f32) + b4_ref[...]
    aux = aux_ref[...]
    dec = jnp.exp(-aux[:, 0:1] * dl_ref[...])
    c = (f[:, :HYENA_WIDTH] * aux[:, 1:2] + f[:, HYENA_WIDTH:] * aux[:, 2:3]) * dec
    o_ref[...] = c

    @pl.when(pl.program_id(0) == 0)
    def _():
        s_ref[...] = jnp.zeros_like(s_ref)

    s_ref[...] += jnp.sum(jnp.abs(c), 0, keepdims=True)


def filter_call(n, tabs, w1p, b1, w2, b2, w3, b3, w4, b4, freq):
    feat, aux, deltas = tabs
    tr = min(2 * n, 1024)
    row = lambda i: (i, 0)
    full = lambda a: pl.BlockSpec(a.shape, lambda i: (0,) * a.ndim)
    ws = [w1p, b1, w2, b2, w3, b3, w4, b4, freq]
    return pl.pallas_call(
        _filt_kernel,
        out_shape=(jax.ShapeDtypeStruct((2 * n, HYENA_WIDTH), jnp.float32),
                   jax.ShapeDtypeStruct((1, HYENA_WIDTH), jnp.float32)),
        grid=(2 * n // tr,),
        in_specs=[pl.BlockSpec((tr, LANE), row), pl.BlockSpec((tr, LANE), row), full(deltas)]
                 + [full(a) for a in ws],
        out_specs=(pl.BlockSpec((tr, HYENA_WIDTH), row),
                   pl.BlockSpec((1, HYENA_WIDTH), lambda i: (0, 0))),
        compiler_params=_cparams(1),
        name="hyena_filter",
    )(feat, aux, deltas, *ws)


def fnet_consts(n, n1):
    cst = dft_consts(n, n1)
    c, s = _cs(FNET_GROUP_DIM, FNET_GROUP_DIM, FNET_GROUP_DIM)
    cst["chan"] = _const_rhs3(np.concatenate([c, s], 0) / math.sqrt(n * FNET_GROUP_DIM))
    return cst


def _fn_slab_kernel(ar_ref, ai_ref, tr_ref, ti_ref, f_ref, cs_ref, o_ref, *, kb, n2):
    f = f_ref[...]
    cs = cs_ref[...]
    gd = FNET_GROUP_DIM
    for j in range(kb):
        ar, ai, tr, ti = ar_ref[j], ai_ref[j], tr_ref[j], ti_ref[j]
        br = ar * tr - ai * ti
        bi = ar * ti + ai * tr
        v = jnp.dot(f, _data_rhs3(jnp.concatenate([br, bi], 0)), preferred_element_type=jnp.float32)
        vr, vi = v[:n2], v[n2:]
        for g in range(FNET_GROUPS):
            lhs = jnp.concatenate([vr[:, g * gd:(g + 1) * gd], vi[:, g * gd:(g + 1) * gd]], 1)
            y = jnp.dot(_data_lhs3(lhs), cs, preferred_element_type=jnp.float32)
            o_ref[:, j * CH + g * gd:j * CH + (g + 1) * gd] = y.astype(o_ref.dtype)


def fn_slab_call(a3, cst, name):
    n1, n2 = cst["n1"], cst["n2"]
    groups = a3.shape[0]
    kb = 8
    a5 = a3.reshape(groups, 2, n1, n2, CH)
    slab = lambda part: pl.BlockSpec((None, None, kb, n2, CH), lambda b, g: (g, part, b, 0, 0))
    tw = pl.BlockSpec((kb, n2, 1), lambda b, g: (b, 0, 0))
    return pl.pallas_call(
        functools.partial(_fn_slab_kernel, kb=kb, n2=n2),
        out_shape=jax.ShapeDtypeStruct((groups, n2, n1 * CH), jnp.bfloat16),
        grid=(n1 // kb, groups),
        in_specs=[slab(0), slab(1), tw, tw,
                  pl.BlockSpec(cst["f2"].shape, lambda b, g: (0, 0)),
                  pl.BlockSpec(cst["chan"].shape, lambda b, g: (0, 0))],
        out_specs=pl.BlockSpec((None, n2, kb * CH), lambda b, g: (g, 0, b)),
        compiler_params=_cparams(2),
        name=name,
    )(a5, a5, cst["tw_r"], cst["tw_i"], cst["f2"], cst["chan"])


def hyena_long(z, x0, n, batch, row_off, cst, tabs, filt_params, d_bias, tag):
    n1, n2 = cst["n1"], cst["n2"]
    w1, b1, w2, b2, w3, b3, w4, b4, freq = filt_params
    w1p = jnp.concatenate([w1, jnp.zeros((LANE - HYENA_EMB, HYENA_FILTER_HIDDEN), w1.dtype)], 0)
    circ, ssum = filter_call(n, tabs, w1p, b1[None, :], w2, b2[None, :], w3, b3[None, :],
                             w4, b4[None, :], freq)
    width = n2 * CH
    h3 = lead_call(cst["lead_real"], circ.reshape(n1, width), n1, 0, 1, "hyena_filt_dft1_" + tag)
    h5 = hy_slab_call(h3, cst, None, "hyena_filt_dft2_" + tag)
    z2d = z.reshape(-1, width)
    x02d = x0.reshape(-1, width)
    pairs = batch // 2
    a3 = lead_call(cst["lead_pair"], z2d, n1, row_off // n2, pairs, "hyena_dft1_" + tag)
    d5 = hy_slab_call(a3, cst, h5, "hyena_conv_" + tag)
    reps = width // CH if width <= 8192 else 8192 // CH
    y = lead_out_call(cst["lead_inv"], d5.reshape(pairs, 2 * n1, width), z2d, x02d,
                      jnp.tile(d_bias[None, :], (1, reps)), jnp.tile(ssum, (1, reps)),
                      n1, row_off // n2, "hyena_idft1_" + tag)
    return y.reshape(batch * n, CH)


def fnet_seq(u, n, batch, row_off, cst, tag):
    n1, n2 = cst["n1"], cst["n2"]
    width = n2 * CH
    a3 = lead_call(cst["lead_real"], u.reshape(-1, width), n1, row_off // n2, batch, "fnet_dft1_" + tag)
    return fn_slab_call(a3, cst, "fnet_dft2_" + tag).reshape(batch * n, CH)


SC_TM = CTX_LEN


def _sconv_kernel(u_ref, p_ref, n_ref, w_ref, b_ref, z_ref, x0_ref, *, lat_tiles, tiles_per_seq):
    i = pl.program_id(0)
    pos = i % tiles_per_seq
    is_lat = i < lat_tiles
    keep_prev = jnp.where(is_lat & (pos != 0), 1.0, 0.0)
    keep_next = jnp.where(is_lat & (pos != tiles_per_seq - 1), 1.0, 0.0)
    u = u_ref[...]
    tm = u.shape[0]
    row = lax.broadcasted_iota(jnp.int32, u.shape, 0)
    um = jnp.where(row == 0, p_ref[7:8, :] * keep_prev, pltpu.roll(u, 1, 0))
    up = jnp.where(row == tm - 1, n_ref[0:1, :] * keep_next, pltpu.roll(u, tm - 1, 0))
    w = w_ref[...]
    uc = b_ref[...] + um * w[0:1] + u * w[1:2] + up * w[2:3]
    x0_ref[...] = uc[:, :CH]
    z_ref[...] = uc[:, CH:2 * CH] * uc[:, 2 * CH:]


def sconv_call(u, conv_w, conv_b, l, n_lat, seq):
    rows = u.shape[0]
    tm = SC_TM
    per8 = tm // 8
    last8 = rows // 8 - 1
    out = jax.ShapeDtypeStruct((rows, CH), jnp.float32)
    return pl.pallas_call(
        functools.partial(_sconv_kernel, lat_tiles=n_lat // tm, tiles_per_seq=seq // tm),
        out_shape=(out, out),
        grid=(rows // tm,),
        in_specs=[pl.BlockSpec((tm, 3 * CH), lambda i: (i, 0)),
                  pl.BlockSpec((8, 3 * CH), lambda i: (jnp.maximum(i * per8 - 1, 0), 0)),
                  pl.BlockSpec((8, 3 * CH), lambda i: (jnp.minimum((i + 1) * per8, last8), 0)),
                  pl.BlockSpec((None, 3, 3 * CH), lambda i: (l, 0, 0)),
                  pl.BlockSpec((None, 1, 3 * CH), lambda i: (l, 0, 0))],
        out_specs=(pl.BlockSpec((tm, CH), lambda i: (i, 0)), pl.BlockSpec((tm, CH), lambda i: (i, 0))),
        compiler_params=_cparams(1),
        name="hyena_short_conv",
    )(u, u, u, conv_w, conv_b)


RANK_TM = 512


def _expert_onehots(r):
    lane = lax.broadcasted_iota(jnp.int32, r.shape, 1).astype(jnp.float32)
    return (jnp.where(lane == r[:, 0:1], 1.0, 0.0), jnp.where(lane == r[:, 1:2], 1.0, 0.0))


def _lanes01(v0, v1, shape):
    lane = lax.broadcasted_iota(jnp.int32, shape, 1)
    return jnp.where(lane == 0, v0, jnp.where(lane == 1, v1, 0.0))


def _rank_kernel(r_ref, o_ref, c_ref, cnt_ref):
    @pl.when(pl.program_id(0) == 0)
    def _():
        cnt_ref[...] = jnp.zeros_like(cnt_ref)

    r = r_ref[...]
    tm = r.shape[0]
    oh0, oh1 = _expert_onehots(r)
    earlier = (lax.broadcasted_iota(jnp.int32, (tm, tm), 1)
               < lax.broadcasted_iota(jnp.int32, (tm, tm), 0))
    tri = jnp.where(earlier, 1.0, 0.0).astype(jnp.bfloat16)
    pre0 = jnp.dot(tri, oh0.astype(jnp.bfloat16), preferred_element_type=jnp.float32)
    pre1 = jnp.dot(tri, oh1.astype(jnp.bfloat16), preferred_element_type=jnp.float32)
    cnt = cnt_ref[...]
    tot0 = jnp.sum(oh0, 0, keepdims=True)
    rank0 = jnp.sum((pre0 + cnt) * oh0, -1, keepdims=True)
    rank1 = jnp.sum((pre1 + cnt + tot0) * oh1, -1, keepdims=True)
    o_ref[...] = _lanes01(rank0, rank1, r.shape)
    cnt_ref[...] = cnt + tot0 + jnp.sum(oh1, 0, keepdims=True)
    c_ref[...] = cnt_ref[...]


def _dest_kernel(r_ref, k_ref, ps_ref, o_ref):
    r = r_ref[...]
    k = k_ref[...]
    oh0, oh1 = _expert_onehots(r)
    ps = ps_ref[...]
    d0 = jnp.sum(oh0 * ps, -1, keepdims=True) + k[:, 0:1]
    d1 = jnp.sum(oh1 * ps, -1, keepdims=True) + k[:, 1:2]
    o_ref[...] = _lanes01(d0, d1, r.shape).astype(jnp.int32)


def _dispatch(route):
    tm = RANK_TM
    row = pl.BlockSpec((tm, ROUTE_LANES), lambda i: (i, 0))
    one = pl.BlockSpec((1, ROUTE_LANES), lambda i: (0, 0))
    rank, counts = pl.pallas_call(
        _rank_kernel,
        out_shape=(jax.ShapeDtypeStruct((NT, ROUTE_LANES), jnp.float32),
                   jax.ShapeDtypeStruct((1, ROUTE_LANES), jnp.float32)),
        grid=(NT // tm,),
        in_specs=[row],
        out_specs=(row, one),
        scratch_shapes=[pltpu.VMEM((1, ROUTE_LANES), jnp.float32)],
        compiler_params=_cparams(1),
        name="moe_rank",
    )(route)
    counts = counts[0].astype(jnp.int32)
    padded = ((counts + MOE_TB - 1) // MOE_TB) * MOE_TB
    lane = jnp.arange(ROUTE_LANES, dtype=jnp.int32)
    pend = jnp.sum(jnp.where(lane[None, :] <= lane[:, None], padded[None, :], 0), 1)
    pstart = (pend - padded).astype(jnp.float32)[None, :]
    dest = pl.pallas_call(
        _dest_kernel,
        out_shape=jax.ShapeDtypeStruct((NT, ROUTE_LANES), jnp.int32),
        grid=(NT // tm,),
        in_specs=[row, row, one],
        out_specs=row,
        compiler_params=_cparams(1),
        name="moe_dest",
    )(route, rank, pstart)
    dest_t = dest[:, :TOP_K].reshape(NT // DISP_TM, DISP_TM, TOP_K).transpose(0, 2, 1)
    pend_e = pend[:N_EXPERTS]
    starts = jnp.arange(MOE_BLOCKS, dtype=jnp.int32) * MOE_TB
    block_e = jnp.minimum(jnp.sum((pend_e[None, :] <= starts[:, None]).astype(jnp.int32), 1),
                          N_EXPERTS - 1)
    n_used = (pend_e[N_EXPERTS - 1] // MOE_TB).astype(jnp.int32).reshape(1)
    return dest_t, block_e, n_used


def kernel(x, c, ctx, c_ctx, w_ada, b_ada, w_in, conv_w, conv_b, hy_w1, hy_b1, hy_w2, hy_b2, hy_w3, hy_b3, hy_w4, hy_b4, hy_freq, hy_dbias, attn_sink, w_br_attn, w_br_hyena, w_br_fnet, w_out, ln1_g, ln1_b, ln2_g, ln2_b, rg_w, rg_b, re_w, re_b, moe_w1, moe_w3, moe_w2):
    bf16 = jnp.bfloat16
    xs = jnp.concatenate([x.reshape(N_LAT, D_MODEL), ctx.reshape(N_CTX, D_MODEL)], 0)
    c8 = jnp.concatenate([c, c_ctx[None, :], jnp.zeros((MOD_ROWS - BATCH - 1, D_MODEL), c.dtype)], 0)
    mod4 = ada_call(c8, w_ada, b_ada).reshape(DEPTH, MOD_ROWS, 1, 6 * D_MODEL)
    cos_t, sin_t = rope_tables(512)
    ln1_g3, ln1_b3 = ln1_g.reshape(DEPTH, 1, D_MODEL), ln1_b.reshape(DEPTH, 1, D_MODEL)
    ln2_g3, ln2_b3 = ln2_g.reshape(DEPTH, 1, D_MODEL), ln2_b.reshape(DEPTH, 1, D_MODEL)
    pad = jnp.zeros((DEPTH, D_MODEL, ROUTE_LANES - N_GROUPS - N_EXPERTS), jnp.float32)
    w_route = jnp.concatenate([rg_w, re_w, pad], -1).astype(bf16)
    b_route = jnp.concatenate([rg_b, re_b, pad[:, 0]], -1).reshape(DEPTH, 1, ROUTE_LANES)

    hy_lat, hy_ctx = dft_consts(2 * SEQ, 64), dft_consts(2 * CTX_LEN, 16)
    fn_lat, fn_ctx = fnet_consts(SEQ, 64), fnet_consts(CTX_LEN, 16)
    tab_lat, tab_ctx = filter_tables(SEQ), filter_tables(CTX_LEN)
    conv_b3 = conv_b.reshape(DEPTH, 1, 3 * HYENA_WIDTH)

    xb = jnp.zeros((MOE_SLOTS, D_MODEL), jnp.float32)
    h = pre_call(xs, mod4)
    for l in range(DEPTH):
        qkv = qkv_call(h, w_in, l, cos_t, sin_t)
        u_hy = proj_call(h, w_in, l, HY_OFF, 3 * HYENA_WIDTH, None, jnp.float32, "proj_hyena")
        u_fn = proj_call(h, w_in, l, FN_OFF, FNET_WIDTH, None, jnp.float32, "proj_fnet")
        gates = proj_call(h, w_in, l, GATE_OFF, 3 * D_MODEL, "sigmoid", bf16, "proj_gates")
        a = attn_call(qkv, attn_sink, l)
        filt = (hy_w1[l], hy_b1[l], hy_w2[l], hy_b2[l], hy_w3[l], hy_b3[l], hy_w4[l], hy_b4[l], hy_freq[l])
        z, x0 = sconv_call(u_hy, conv_w, conv_b3, l, N_LAT, SEQ)
        yh_l = hyena_long(z, x0, SEQ, BATCH, 0, hy_lat, tab_lat, filt, hy_dbias[l], "lat")
        yh_c = hyena_long(z, x0, CTX_LEN, BATCH, N_LAT, hy_ctx, tab_ctx, filt, hy_dbias[l], "ctx")
        yf_l = fnet_seq(u_fn, SEQ, BATCH, 0, fn_lat, "lat")
        yf_c = fnet_seq(u_fn, CTX_LEN, BATCH, N_LAT, fn_ctx, "ctx")
        xs, t = mix_call(a, yh_l, yh_c, yf_l, yf_c, gates, xs,
                         w_br_attn[l].astype(bf16), w_br_hyena[l].astype(bf16),
                         w_br_fnet[l].astype(bf16), w_out[l].astype(bf16),
                         mod4, ln1_g3, ln1_b3, l)
        route = route_call(t, w_route[l], b_route[l])
        dest_t, block_e, n_used = _dispatch(route)
        xb = disp_call(dest_t, t, xb)
        yb = moe_call(block_e, n_used, xb, moe_w1, moe_w3, moe_w2, l)
        if l < DEPTH - 1:
            xs, h = post_call(xs, yb, dest_t, route, mod4, ln2_g3, ln2_b3, l)
    return post_call(xs, yb, dest_t, route, mod4, ln2_g3, ln2_b3, DEPTH - 1).reshape(BATCH, SEQ, D_MODEL)
```
